```python
import math
import jax, jax.numpy as jnp
from jax import lax
import numpy as np

D_MODEL = 1024
BATCH = 4
SEQ = 8192
DEPTH = 4

GRID_W = 64
CTX_LEN = 256
N_MIXERS = 2
NORM_EPS = 1e-6

MLA_HEADS = 8
QK_NOPE = 128
QK_ROPE = 64
V_DIM = 128
Q_LORA = 256
KV_LORA = 128
MLA_WIDTH = MLA_HEADS * V_DIM
MLA_IN_WIDTH = Q_LORA + KV_LORA + QK_ROPE + MLA_WIDTH
ROPE_AXIS_DIM = QK_ROPE // 2
ROPE_THETA = 10000.0
Q_BLOCK = 128

FOURIER_WIDTH = D_MODEL
FOURIER_GROUPS = 8
FOURIER_GROUP_DIM = FOURIER_WIDTH // FOURIER_GROUPS

N_MLA_LAYERS = (DEPTH + 1) // 2
N_FOURIER_LAYERS = DEPTH // 2
ADA_STD = 0.5 * D_MODEL ** -0.5

kernel_name = "hybrid_mla_fourier_gated_prefix_dit"


def rms_norm(x, g):
    xf = x.astype(jnp.float32)
    y = xf * lax.rsqrt(jnp.mean(xf * xf, axis=-1, keepdims=True) + NORM_EPS)
    return (y * g.astype(jnp.float32)).astype(x.dtype)


def axial_rope_tables(row, col):
    inv_freq = 1.0 / (ROPE_THETA ** (jnp.arange(0, ROPE_AXIS_DIM, 2, dtype=jnp.float32) / ROPE_AXIS_DIM))
    ang_r = row.astype(jnp.float32)[:, None] * inv_freq[None, :]
    ang_c = col.astype(jnp.float32)[:, None] * inv_freq[None, :]
    return jnp.cos(ang_r), jnp.sin(ang_r), jnp.cos(ang_c), jnp.sin(ang_c)


def _rotate(t, cos, sin):
    t1, t2 = jnp.split(t, 2, axis=-1)
    return jnp.concatenate([t1 * cos - t2 * sin, t2 * cos + t1 * sin], axis=-1)


def apply_axial_rope(t, tables):
    cos_r, sin_r, cos_c, sin_c = [a.astype(t.dtype)[None, :, None, :] for a in tables]
    t_r, t_c = jnp.split(t, 2, axis=-1)
    return jnp.concatenate([_rotate(t_r, cos_r, sin_r), _rotate(t_c, cos_c, sin_c)], axis=-1)


def mla_query(c_q, g_qa, w_qup):
    b, t, _ = c_q.shape
    q = (rms_norm(c_q, g_qa) @ w_qup).reshape(b, t, MLA_HEADS, QK_NOPE + QK_ROPE)
    return q[..., :QK_NOPE], q[..., QK_NOPE:]


def mla_keyvalue(c_kv, g_kva, w_kvup):
    b, t, _ = c_kv.shape
    kv = (rms_norm(c_kv, g_kva) @ w_kvup).reshape(b, t, MLA_HEADS, QK_NOPE + V_DIM)
    return kv[..., :QK_NOPE], kv[..., QK_NOPE:]


def assemble_keys(k_nope, k_pe):
    return jnp.concatenate([k_nope, jnp.broadcast_to(k_pe, k_nope.shape[:-1] + (QK_ROPE,))], axis=-1)


def dense_attention(q, k, v):
    scale = 1.0 / math.sqrt(QK_NOPE + QK_ROPE)
    s = jnp.einsum('bqhd,bkhd->bhqk', q, k).astype(jnp.float32) * scale
    p = jax.nn.softmax(s, axis=-1).astype(v.dtype)
    return jnp.einsum('bhqk,bkhd->bqhd', p, v)


def latent_attention(q, k_lat, v_lat, k_ctx, v_ctx):
    b, t, h, dk = q.shape
    k_all = jnp.concatenate([k_ctx, k_lat], axis=1)
    v_all = jnp.concatenate([v_ctx, v_lat], axis=1)
    nb = t // Q_BLOCK
    qb = q.reshape(b, nb, Q_BLOCK, h, dk).transpose(1, 0, 2, 3, 4)
    o = lax.map(lambda qblk: dense_attention(qblk, k_all, v_all), qb)
    return o.transpose(1, 0, 2, 3, 4).reshape(b, t, h * V_DIM)


def fourier_mix(u, w_in, w_out):
    proj = u @ w_in
    z, gate = proj[..., :FOURIER_WIDTH], proj[..., FOURIER_WIDTH:]
    b, t, _ = z.shape
    zg = z.reshape(b, t, FOURIER_GROUPS, FOURIER_GROUP_DIM).astype(jnp.float32)
    y = jnp.fft.fftn(zg, axes=(1, 3), norm="ortho").real.astype(u.dtype).reshape(b, t, FOURIER_WIDTH)
    return (y * jax.nn.silu(gate)) @ w_out


def ada_params(cond_act, w, b):
    mod = cond_act @ w + b
    return jnp.split(mod, 3, axis=-1)


def setup_inputs(seed: int = 0) -> dict:
    key = jax.random.key(seed)
    ks = jax.random.split(key, 16)
    n = jax.random.normal
    f32 = jnp.float32
    return {
        "x": n(ks[0], (BATCH, SEQ, D_MODEL), f32),
        "c": n(ks[1], (BATCH, D_MODEL), f32),
        "ctx": n(ks[2], (BATCH, CTX_LEN, D_MODEL), f32),
        "c_ctx": n(ks[3], (D_MODEL,), f32),
        "norm_g": 1.0 + 0.02 * n(ks[4], (DEPTH, D_MODEL), f32),
        "w_ada": ADA_STD * n(ks[5], (DEPTH, D_MODEL, 3 * D_MODEL), f32),
        "b_ada": 0.02 * n(ks[6], (DEPTH, 3 * D_MODEL), f32),
        "mla_w_in": D_MODEL ** -0.5 * n(ks[7], (N_MLA_LAYERS, D_MODEL, MLA_IN_WIDTH), f32),
        "mla_g_qa": 1.0 + 0.02 * n(ks[8], (N_MLA_LAYERS, Q_LORA), f32),
        "mla_w_qup": Q_LORA ** -0.5 * n(ks[9], (N_MLA_LAYERS, Q_LORA, MLA_HEADS * (QK_NOPE + QK_ROPE)), f32),
        "mla_g_kva": 1.0 + 0.02 * n(ks[10], (N_MLA_LAYERS, KV_LORA), f32),
        "mla_w_kvup": KV_LORA ** -0.5 * n(ks[11], (N_MLA_LAYERS, KV_LORA, MLA_HEADS * (QK_NOPE + V_DIM)), f32),
        "mla_w_out": MLA_WIDTH ** -0.5 * n(ks[12], (N_MLA_LAYERS, MLA_WIDTH, D_MODEL), f32),
        "fno_w_in": D_MODEL ** -0.5 * n(ks[13], (N_FOURIER_LAYERS, D_MODEL, 2 * FOURIER_WIDTH), f32),
        "fno_w_out": FOURIER_WIDTH ** -0.5 * n(ks[14], (N_FOURIER_LAYERS, FOURIER_WIDTH, D_MODEL), f32),
        "final_g": 1.0 + 0.02 * n(ks[15], (D_MODEL,), f32),
    }


def reference(x, c, ctx, c_ctx, norm_g, w_ada, b_ada, mla_w_in, mla_g_qa, mla_w_qup, mla_g_kva,
              mla_w_kvup, mla_w_out, fno_w_in, fno_w_out, final_g):
    n_tok = x.shape[1]
    rows = n_tok // GRID_W
    row = jnp.repeat(jnp.arange(rows, dtype=jnp.int32), GRID_W)
    col = jnp.tile(jnp.arange(GRID_W, dtype=jnp.int32), rows)
    rope = axial_rope_tables(row, col)

    act_lat = jax.nn.silu(c)
    act_ctx = jax.nn.silu(c_ctx)
    h_lat, h_ctx = x, ctx
    kv_lo, kv_hi = Q_LORA, Q_LORA + KV_LORA
    pe_hi = kv_hi + QK_ROPE

    for i in range(DEPTH):
        mixer = i % N_MIXERS
        idx = i // N_MIXERS
        ctx_later = any(j % N_MIXERS == 0 for j in range(i + 1, DEPTH))

        sh_l, sc_l, gt_l = ada_params(act_lat, w_ada[i], b_ada[i])
        u_lat = rms_norm(h_lat, norm_g[i]) * (1.0 + sc_l[:, None, :]) + sh_l[:, None, :]
        sh_c, sc_c, gt_c = ada_params(act_ctx, w_ada[i], b_ada[i])

        if mixer == 0:
            w_in = mla_w_in[idx]
            u_ctx = rms_norm(h_ctx, norm_g[i]) * (1.0 + sc_c) + sh_c
            p_lat = u_lat @ w_in
            qn_l, qp_l = mla_query(p_lat[..., :kv_lo], mla_g_qa[idx], mla_w_qup[idx])
            kn_l, v_l = mla_keyvalue(p_lat[..., kv_lo:kv_hi], mla_g_kva[idx], mla_w_kvup[idx])
            kp_l = apply_axial_rope(p_lat[..., None, kv_hi:pe_hi], rope)
            q_l = jnp.concatenate([qn_l, apply_axial_rope(qp_l, rope)], axis=-1)
            k_l = assemble_keys(kn_l, kp_l)
            if ctx_later:
                p_ctx = u_ctx @ w_in
            else:
                p_ctx = u_ctx @ w_in[:, kv_lo:pe_hi]
                p_ctx = jnp.concatenate([jnp.zeros(p_ctx.shape[:-1] + (kv_lo,), p_ctx.dtype), p_ctx], axis=-1) if False else p_ctx
            off = 0 if not ctx_later else kv_lo
            kn_c, v_c = mla_keyvalue(p_ctx[..., off:off + KV_LORA], mla_g_kva[idx], mla_w_kvup[idx])
            k_c = assemble_keys(kn_c, p_ctx[..., None, off + KV_LORA:off + KV_LORA + QK_ROPE])

            o_lat = latent_attention(q_l, k_l, v_l, k_c, v_c)
            y_lat = (o_lat * jax.nn.silu(p_lat[..., pe_hi:])) @ mla_w_out[idx]
            if ctx_later:
                qn_c, qp_c = mla_query(p_ctx[..., :kv_lo], mla_g_qa[idx], mla_w_qup[idx])
                q_c = jnp.concatenate([qn_c, qp_c], axis=-1)
                b_, t_c = h_ctx.shape[0], h_ctx.shape[1]
                o_ctx = dense_attention(q_c, k_c, v_c).reshape(b_, t_c, MLA_WIDTH)
                y_ctx = (o_ctx * jax.nn.silu(p_ctx[..., pe_hi:])) @ mla_w_out[idx]
                h_ctx = h_ctx + gt_c * y_ctx
            h_lat = h_lat + gt_l[:, None, :] * y_lat
        else:
            y_lat = fourier_mix(u_lat, fno_w_in[idx], fno_w_out[idx])
            if ctx_later:
                u_ctx = rms_norm(h_ctx, norm_g[i]) * (1.0 + sc_c) + sh_c
                h_ctx = h_ctx + gt_c * fourier_mix(u_ctx, fno_w_in[idx], fno_w_out[idx])
            h_lat = h_lat + gt_l[:, None, :] * y_lat

    return rms_norm(h_lat, final_g)
```

```python
import functools
import math

import numpy as np
import jax
import jax.numpy as jnp
from jax import lax
from jax.experimental import pallas as pl
from jax.experimental.pallas import tpu as pltpu

F32 = jnp.float32
BF16 = jnp.bfloat16

D_MODEL = 1024
DEPTH = 4
GRID_W = 64
NORM_EPS = 1e-6
HEADS = 8
QK_NOPE = 128
QK_ROPE = 64
V_DIM = 128
Q_LORA = 256
KV_LORA = 128
ROPE_THETA = 10000.0
ROPE_QUARTER = QK_ROPE // 4
QK_PAD = 256
FOURIER_GROUPS = 8
GROUP_DIM = D_MODEL // FOURIER_GROUPS
DFT_INNER = 128
VMEM_LIMIT = 56 * 1024 * 1024

NT_DIMS = (((1,), (1,)), ((), ()))
TN_DIMS = (((0,), (0,)), ((), ()))


def _params(*semantics):
    return pltpu.CompilerParams(dimension_semantics=semantics, vmem_limit_bytes=VMEM_LIMIT)


def _silu(x):
    return x * jax.nn.sigmoid(x)


def _rms(x):
    return x * lax.rsqrt(jnp.mean(x * x, axis=-1, keepdims=True) + NORM_EPS)


def _modulated_norm(h_ref, g_ref, sc_ref, sh_ref):
    u = (_rms(h_ref[0]) * g_ref[...]) * (1.0 + sc_ref[0]) + sh_ref[0]
    return u.astype(BF16)


def _ada_kernel(c_ref, w_ref, b_ref, o_ref):
    act = _silu(c_ref[...])
    o_ref[0] = jnp.dot(act, w_ref[0], precision=lax.Precision.HIGHEST,
                       preferred_element_type=F32) + b_ref[0]


def _ada(cond, w_ada, b_ada):
    depth, d, d3 = w_ada.shape
    rows = cond.shape[0]
    return pl.pallas_call(
        _ada_kernel,
        grid=(depth, d3 // d),
        in_specs=[
            pl.BlockSpec((rows, d), lambda i, j: (0, 0)),
            pl.BlockSpec((1, d, d), lambda i, j: (i, 0, j)),
            pl.BlockSpec((1, 1, d), lambda i, j: (i, 0, j)),
        ],
        out_specs=pl.BlockSpec((1, rows, d), lambda i, j: (i, 0, j)),
        out_shape=jax.ShapeDtypeStruct((depth, rows, d3), F32),
        compiler_params=_params("arbitrary", "arbitrary"),
        name="ada",
    )(cond, w_ada, b_ada.reshape(depth, 1, d3))


def _mla_pre_kernel(h_ref, g_ref, sc_ref, sh_ref, wa_ref, wgt_ref, gqa_ref, wqt_ref, gkva_ref,
                    wk_ref, wvt_ref, ropet_ref, cosk_ref, sink_ref,
                    qt_ref, k_ref, vt_ref, gt_ref, *, q_scale):
    ub = _modulated_norm(h_ref, g_ref, sc_ref, sh_ref)
    pa = jnp.dot(ub, wa_ref[...], preferred_element_type=F32)
    c_q = pa[:, :Q_LORA]
    c_kv = pa[:, Q_LORA:Q_LORA + KV_LORA]
    kpe = pa[:, 384:512]
    kpe_rot = pa[:, 512:640]

    gate_t = lax.dot_general(wgt_ref[...], ub, NT_DIMS, preferred_element_type=F32)
    gt_ref[0] = _silu(gate_t).astype(BF16)

    cqn = (_rms(c_q) * gqa_ref[...]).astype(BF16)
    q_t = lax.dot_general(wqt_ref[...], cqn, NT_DIMS, preferred_element_type=F32)
    rope_t = ropet_ref[...]
    r = ROPE_QUARTER
    cos_r, sin_r, cos_c, sin_c = (rope_t[i * r:(i + 1) * r] for i in range(4))
    zeros = jnp.zeros((QK_PAD - QK_NOPE - QK_ROPE, q_t.shape[1]), BF16)
    for hd in range(HEADS):
        b0 = hd * QK_PAD
        qt_ref[0, hd, 0:QK_NOPE, :] = (q_t[b0:b0 + QK_NOPE] * q_scale).astype(BF16)
        p0 = b0 + QK_NOPE
        a, b, c, d = (q_t[p0 + i * r:p0 + (i + 1) * r] for i in range(4))
        rot = jnp.concatenate([a * cos_r - b * sin_r, b * cos_r + a * sin_r,
                               c * cos_c - d * sin_c, d * cos_c + c * sin_c], axis=0)
        qt_ref[0, hd, QK_NOPE:QK_NOPE + QK_ROPE, :] = (rot * q_scale).astype(BF16)
        qt_ref[0, hd, QK_NOPE + QK_ROPE:QK_PAD, :] = zeros

    ckvn = (_rms(c_kv) * gkva_ref[...]).astype(BF16)
    k_nope = jnp.dot(ckvn, wk_ref[...], preferred_element_type=F32)
    v_t = lax.dot_general(wvt_ref[...], ckvn, NT_DIMS, preferred_element_type=F32)
    k_rope = (kpe * cosk_ref[...] + kpe_rot * sink_ref[...]).astype(BF16)
    for hd in range(HEADS):
        k_ref[0, hd, :, 0:QK_NOPE] = k_nope[:, hd * QK_NOPE:(hd + 1) * QK_NOPE].astype(BF16)
        k_ref[0, hd, :, QK_NOPE:QK_PAD] = k_rope
        vt_ref[0, hd] = v_t[hd * V_DIM:(hd + 1) * V_DIM].astype(BF16)


def _mla_pre(h, g, sc, sh, wts, tables, tm):
    bsz, t, d = h.shape
    wa, wgt, gqa, wqt, gkva, wk, wvt = wts
    rope_t, cosk, sink = tables
    q_scale = math.log2(math.e) / math.sqrt(QK_NOPE + QK_ROPE)
    const = lambda shape: pl.BlockSpec(shape, lambda b, i: (0,) * len(shape))
    vec = pl.BlockSpec((1, 1, d), lambda b, i: (b, 0, 0))
    return pl.pallas_call(
        functools.partial(_mla_pre_kernel, q_scale=q_scale),
        grid=(bsz, t // tm),
        in_specs=[
            pl.BlockSpec((1, tm, d), lambda b, i: (b, i, 0)),
            const((1, d)), vec, vec,
            const(wa.shape), const(wgt.shape), const(gqa.shape), const(wqt.shape),
            const(gkva.shape), const(wk.shape), const(wvt.shape),
            pl.BlockSpec((QK_ROPE, tm), lambda b, i: (0, i)),
            pl.BlockSpec((tm, 128), lambda b, i: (i, 0)),
            pl.BlockSpec((tm, 128), lambda b, i: (i, 0)),
        ],
        out_specs=[
            pl.BlockSpec((1, HEADS, QK_PAD, tm), lambda b, i: (b, 0, 0, i)),
            pl.BlockSpec((1, HEADS, tm, QK_PAD), lambda b, i: (b, 0, i, 0)),
            pl.BlockSpec((1, HEADS, V_DIM, tm), lambda b, i: (b, 0, 0, i)),
            pl.BlockSpec((1, HEADS * V_DIM, tm), lambda b, i: (b, 0, i)),
        ],
        out_shape=[
            jax.ShapeDtypeStruct((bsz, HEADS, QK_PAD, t), BF16),
            jax.ShapeDtypeStruct((bsz, HEADS, t, QK_PAD), BF16),
            jax.ShapeDtypeStruct((bsz, HEADS, V_DIM, t), BF16),
            jax.ShapeDtypeStruct((bsz, HEADS * V_DIM, t), BF16),
        ],
        compiler_params=_params("arbitrary", "arbitrary"),
        name="mla_pre",
    )(h, g, sc, sh, wa, wgt, gqa, wqt, gkva, wk, wvt, rope_t, cosk, sink)


def _attn_chunk(kc, vc, q_t, m_sc, l_sc, acc_sc, first):
    s = jnp.dot(kc, q_t, preferred_element_type=F32)
    mc = jnp.max(s, axis=0, keepdims=True)
    if first:
        m_new = mc
        p = jnp.exp2(s - m_new)
        l_sc[...] = jnp.sum(p, axis=0, keepdims=True)
        acc_sc[...] = jnp.dot(vc, p.astype(BF16), preferred_element_type=F32)
    else:
        m_old = m_sc[...]
        m_new = jnp.maximum(m_old, mc)
        alpha = jnp.exp2(m_old - m_new)
        p = jnp.exp2(s - m_new)
        l_sc[...] = alpha * l_sc[...] + jnp.sum(p, axis=0, keepdims=True)
        acc_sc[...] = alpha * acc_sc[...] + jnp.dot(vc, p.astype(BF16), preferred_element_type=F32)
    m_sc[...] = m_new


def _attn_kernel(*refs, n_src, kb):
    qt_ref = refs[0]
    srcs = refs[1:1 + 2 * n_src]
    o_ref = refs[1 + 2 * n_src]
    m_sc, l_sc, acc_sc = refs[2 + 2 * n_src:]
    q_t = qt_ref[0, 0]
    for s in range(n_src):
        k_ref, vt_ref = srcs[2 * s], srcs[2 * s + 1]
        tk = k_ref.shape[2]
        ck = min(kb, tk)
        n_chunks = tk // ck
        start = 0
        if s == 0:
            _attn_chunk(k_ref[0, 0, 0:ck, :], vt_ref[0, 0, :, 0:ck], q_t, m_sc, l_sc, acc_sc, True)
            start = 1
        if n_chunks > start:
            def body(i, carry, k_ref=k_ref, vt_ref=vt_ref, ck=ck):
                off = pl.multiple_of(i * ck, ck)
                _attn_chunk(k_ref[0, 0, pl.ds(off, ck), :], vt_ref[0, 0, :, pl.ds(off, ck)],
                            q_t, m_sc, l_sc, acc_sc, False)
                return carry
            lax.fori_loop(start, n_chunks, body, 0)
    o_ref[0, 0] = (acc_sc[...] / l_sc[...]).astype(o_ref.dtype)


def _attention(q_t, sources, qb, kb):
    bsz, heads, _, t = q_t.shape
    in_specs = [pl.BlockSpec((1, 1, QK_PAD, qb), lambda b, h, i: (b, h, 0, i))]
    args = [q_t]
    for k, v_t in sources:
        tk = k.shape[2]
        in_specs.append(pl.BlockSpec((1, 1, tk, QK_PAD), lambda b, h, i: (b, h, 0, 0)))
        in_specs.append(pl.BlockSpec((1, 1, V_DIM, tk), lambda b, h, i: (b, h, 0, 0)))
        args += [k, v_t]
    return pl.pallas_call(
        functools.partial(_attn_kernel, n_src=len(sources), kb=kb),
        grid=(bsz, heads, t // qb),
        in_specs=in_specs,
        out_specs=pl.BlockSpec((1, 1, V_DIM, qb), lambda b, h, i: (b, h, 0, i)),
        out_shape=jax.ShapeDtypeStruct((bsz, heads, V_DIM, t), BF16),
        scratch_shapes=[pltpu.VMEM((1, qb), F32), pltpu.VMEM((1, qb), F32), pltpu.VMEM((V_DIM, qb), F32)],
        compiler_params=_params("arbitrary", "arbitrary", "arbitrary"),
        name="attention",
    )(*args)


def _mla_post_kernel(ot_ref, gt_ref, w_ref, ga_ref, h_ref, out_ref):
    og = (ot_ref[0].astype(F32) * gt_ref[0].astype(F32)).astype(BF16)
    y = lax.dot_general(og, w_ref[...], TN_DIMS, preferred_element_type=F32)
    out_ref[0] = h_ref[0] + ga_ref[0] * y


def _mla_post(o_t, g_t, w_out, ga, h, tm):
    bsz, t, d = h.shape
    return pl.pallas_call(
        _mla_post_kernel,
        grid=(bsz, t // tm),
        in_specs=[
            pl.BlockSpec((1, d, tm), lambda b, i: (b, 0, i)),
            pl.BlockSpec((1, d, tm), lambda b, i: (b, 0, i)),
            pl.BlockSpec((d, d), lambda b, i: (0, 0)),
            pl.BlockSpec((1, 1, d), lambda b, i: (b, 0, 0)),
            pl.BlockSpec((1, tm, d), lambda b, i: (b, i, 0)),
        ],
        out_specs=pl.BlockSpec((1, tm, d), lambda b, i: (b, i, 0)),
        out_shape=jax.ShapeDtypeStruct((bsz, t, d), F32),
        compiler_params=_params("arbitrary", "arbitrary"),
        name="mla_post",
    )(o_t, g_t, w_out, ga, h)


def _fno_pre_kernel(h_ref, g_ref, sc_ref, sh_ref, w_ref, cs_ref, wre_ref, wim_ref, sg_ref):
    ub = _modulated_norm(h_ref, g_ref, sc_ref, sh_ref)
    proj = jnp.dot(ub, w_ref[...], preferred_element_type=F32)
    d = sg_ref.shape[2]
    sg_ref[0] = _silu(proj[:, d:]).astype(BF16)
    zb = proj[:, :d].astype(BF16)
    for g in range(FOURIER_GROUPS):
        lo, hi = g * GROUP_DIM, (g + 1) * GROUP_DIM
        ab = jnp.dot(zb[:, lo:hi], cs_ref[...], preferred_element_type=F32)
        wre_ref[0, :, lo:hi] = ab[:, :GROUP_DIM].astype(BF16)
        wim_ref[0, :, lo:hi] = ab[:, GROUP_DIM:].astype(BF16)


def _fno_pre(h, g, sc, sh, w_in, cs, tm):
    bsz, t, d = h.shape
    vec = pl.BlockSpec((1, 1, d), lambda b, i: (b, 0, 0))
    tok = pl.BlockSpec((1, tm, d), lambda b, i: (b, i, 0))
    out = jax.ShapeDtypeStruct((bsz, t, d), BF16)
    return pl.pallas_call(
        _fno_pre_kernel,
        grid=(bsz, t // tm),
        in_specs=[tok, pl.BlockSpec((1, d), lambda b, i: (0, 0)), vec, vec,
                  pl.BlockSpec(w_in.shape, lambda b, i: (0, 0)),
                  pl.BlockSpec(cs.shape, lambda b, i: (0, 0))],
        out_specs=[tok, tok, tok],
        out_shape=[out, out, out],
        compiler_params=_params("arbitrary", "arbitrary"),
        name="fno_pre",
    )(h, g, sc, sh, w_in, cs)


def _dft1_kernel(xre_ref, xim_ref, f_ref, twc_ref, tws_ref, zre_ref, zim_ref, *, n_t1, cols):
    x = jnp.concatenate([xre_ref[0], xim_ref[0]], axis=0)
    z = jnp.dot(f_ref[...], x, preferred_element_type=F32)
    n2 = z.shape[0] // 2
    zr, zi = z[:n2], z[n2:]
    for j in range(n_t1):
        c, s = twc_ref[j], tws_ref[j]
        a, b = zr[:, j * cols:(j + 1) * cols], zi[:, j * cols:(j + 1) * cols]
        zre_ref[0, :, j * cols:(j + 1) * cols] = (a * c - b * s).astype(BF16)
        zim_ref[0, :, j * cols:(j + 1) * cols] = (a * s + b * c).astype(BF16)


def _dft1(wre, wim, f1, twc, tws, n_t1):
    bsz, t, d = wre.shape
    n2 = t // DFT_INNER
    xre = wre.reshape(bsz, n2, DFT_INNER * d)
    xim = wim.reshape(bsz, n2, DFT_INNER * d)
    tc = n_t1 * d
    blk = pl.BlockSpec((1, n2, tc), lambda b, i: (b, 0, i))
    tw = pl.BlockSpec((n_t1, n2, 1), lambda b, i: (i, 0, 0))
    out = jax.ShapeDtypeStruct(xre.shape, BF16)
    return pl.pallas_call(
        functools.partial(_dft1_kernel, n_t1=n_t1, cols=d),
        grid=(bsz, DFT_INNER // n_t1),
        in_specs=[blk, blk, pl.BlockSpec(f1.shape, lambda b, i: (0, 0)), tw, tw],
        out_specs=[blk, blk],
        out_shape=[out, out],
        compiler_params=_params("arbitrary", "arbitrary"),
        name="dft1",
    )(xre, xim, f1, twc, tws)


def _dft2_post_kernel(zre_ref, zim_ref, f_ref, sg_ref, w_ref, ga_ref, h_ref, fg_ref, out_ref,
                      *, kk, scale, final):
    d = w_ref.shape[0]
    for j in range(kk):
        zc = jnp.concatenate([zre_ref[0, j], zim_ref[0, j]], axis=0)
        y = jnp.dot(f_ref[...], zc, preferred_element_type=F32) * scale
        sg = sg_ref[0, :, j * d:(j + 1) * d].astype(F32)
        o = jnp.dot((y * sg).astype(BF16), w_ref[...], preferred_element_type=F32)
        hn = h_ref[0, :, j * d:(j + 1) * d] + ga_ref[0] * o
        if final:
            hn = _rms(hn) * fg_ref[...]
        out_ref[0, :, j * d:(j + 1) * d] = hn


def _dft2_post(zre, zim, f2, sg, w_out, ga, h, final_g, n1, kk, final):
    bsz, t, d = h.shape
    n2 = t // n1
    zre = zre.reshape(bsz, n2, n1, d)
    zim = zim.reshape(bsz, n2, n1, d)
    zblk = pl.BlockSpec((1, kk, n1, d), lambda b, i: (b, i, 0, 0))
    tok = pl.BlockSpec((1, n1, kk * d), lambda b, i: (b, 0, i))
    scale = 1.0 / math.sqrt(t * GROUP_DIM)
    out = pl.pallas_call(
        functools.partial(_dft2_post_kernel, kk=kk, scale=scale, final=final),
        grid=(bsz, n2 // kk),
        in_specs=[zblk, zblk, pl.BlockSpec(f2.shape, lambda b, i: (0, 0)), tok,
                  pl.BlockSpec((d, d), lambda b, i: (0, 0)),
                  pl.BlockSpec((1, 1, d), lambda b, i: (b, 0, 0)), tok,
                  pl.BlockSpec((1, d), lambda b, i: (0, 0))],
        out_specs=tok,
        out_shape=jax.ShapeDtypeStruct((bsz, n1, n2 * d), F32),
        compiler_params=_params("arbitrary", "arbitrary"),
        name="dft2_post",
    )(zre, zim, f2, sg.reshape(bsz, n1, n2 * d), w_out, ga, h.reshape(bsz, n1, n2 * d), final_g)
    return out.reshape(bsz, t, d)


def _dft_cos_sin(n):
    idx = np.arange(n)
    ang = 2.0 * np.pi * ((idx[:, None] * idx[None, :]) % n) / n
    return np.cos(ang), np.sin(ang)


def _dft_tables(t):
    cn, sn = _dft_cos_sin(GROUP_DIM)
    cs = np.concatenate([cn, sn], axis=1)
    if t <= 256:
        n1, n2 = t, 1
    else:
        n1, n2 = DFT_INNER, t // DFT_INNER
    c1, s1 = _dft_cos_sin(n1)
    f2 = np.concatenate([c1, -s1], axis=1)
    c2, s2 = _dft_cos_sin(n2)
    f1 = np.block([[c2, -s2], [s2, c2]])
    k2 = np.arange(n2)[None, :, None]
    t1 = np.arange(n1)[:, None, None]
    ang = 2.0 * np.pi * ((k2 * t1) % t) / t
    as_f32 = lambda a: jnp.asarray(a.astype(np.float32))
    return dict(n1=n1, n2=n2, cs=as_f32(cs).astype(BF16), f1=as_f32(f1).astype(BF16),
                f2=as_f32(f2).astype(BF16), twc=as_f32(np.cos(ang)), tws=as_f32(np.sin(ang)))


def _rope_tables(n_tok):
    inv_freq = 1.0 / (ROPE_THETA ** (jnp.arange(0, 2 * ROPE_QUARTER, 2, dtype=F32) / (2 * ROPE_QUARTER)))
    pos = jnp.arange(n_tok, dtype=jnp.int32)
    ang_r = (pos // GRID_W).astype(F32)[:, None] * inv_freq[None, :]
    ang_c = (pos % GRID_W).astype(F32)[:, None] * inv_freq[None, :]
    return jnp.cos(ang_r), jnp.sin(ang_r), jnp.cos(ang_c), jnp.sin(ang_c)


def _pack_rope(cos_r, sin_r, cos_c, sin_c):
    n_tok = cos_r.shape[0]
    pad = jnp.zeros((n_tok, 128 - QK_ROPE), F32)
    rope_t = jnp.concatenate([cos_r, sin_r, cos_c, sin_c], axis=1).T
    cosk = jnp.concatenate([cos_r, cos_r, cos_c, cos_c, pad], axis=1)
    sink = jnp.concatenate([sin_r, sin_r, sin_c, sin_c, pad], axis=1)
    return rope_t, cosk, sink


def _mla_weights(w_in, g_qa, w_qup, g_kva, w_kvup, w_out):
    d = w_in.shape[0]
    kv_hi = Q_LORA + KV_LORA
    pe_hi = kv_hi + QK_ROPE
    w_pe = w_in[:, kv_hi:pe_hi]
    r = ROPE_QUARTER
    w_rot = jnp.concatenate([-w_pe[:, r:2 * r], w_pe[:, 0:r], -w_pe[:, 3 * r:4 * r], w_pe[:, 2 * r:3 * r]], axis=1)
    zpad = jnp.zeros((d, 128 - QK_ROPE), w_in.dtype)
    wa = jnp.concatenate([w_in[:, :kv_hi], w_pe, zpad, w_rot, zpad], axis=1).astype(BF16)
    wgt = w_in[:, pe_hi:].T.astype(BF16)
    wq = w_qup.reshape(Q_LORA, HEADS, QK_NOPE + QK_ROPE)
    wq = jnp.pad(wq, ((0, 0), (0, 0), (0, QK_PAD - QK_NOPE - QK_ROPE)))
    wqt = wq.reshape(Q_LORA, HEADS * QK_PAD).T.astype(BF16)
    wkv = w_kvup.reshape(KV_LORA, HEADS, QK_NOPE + V_DIM)
    wk = wkv[:, :, :QK_NOPE].reshape(KV_LORA, HEADS * QK_NOPE).astype(BF16)
    wvt = wkv[:, :, QK_NOPE:].reshape(KV_LORA, HEADS * V_DIM).T.astype(BF16)
    return (wa, wgt, g_qa.reshape(1, Q_LORA), wqt, g_kva.reshape(1, KV_LORA), wk, wvt), w_out.astype(BF16)


def kernel(x, c, ctx, c_ctx, norm_g, w_ada, b_ada, mla_w_in, mla_g_qa, mla_w_qup, mla_g_kva,
           mla_w_kvup, mla_w_out, fno_w_in, fno_w_out, final_g):
    bsz, n_tok, d = x.shape
    n_ctx = ctx.shape[1]
    depth = norm_g.shape[0]

    rows = 8
    cond = jnp.concatenate([c, c_ctx[None, :], jnp.zeros((rows - bsz - 1, d), F32)], axis=0)
    mods = _ada(cond, w_ada, b_ada)

    rope_lat = _pack_rope(*_rope_tables(n_tok))
    ones, zero = jnp.ones((n_ctx, ROPE_QUARTER), F32), jnp.zeros((n_ctx, ROPE_QUARTER), F32)
    rope_ctx = _pack_rope(ones, zero, ones, zero)
    dft_lat = _dft_tables(n_tok)
    dft_ctx = _dft_tables(n_ctx)
    fg = final_g.reshape(1, d)

    h_lat, h_ctx = x, ctx
    for i in range(depth):
        mixer, idx = i % 2, i // 2
        ctx_later = any(j % 2 == 0 for j in range(i + 1, depth))
        last = i == depth - 1
        g = norm_g[i].reshape(1, d)
        sh, sc, ga = (mods[i, :, j * d:(j + 1) * d] for j in range(3))
        lat = lambda a: a[:bsz].reshape(bsz, 1, d)
        cx = lambda a: jnp.broadcast_to(a[bsz].reshape(1, 1, d), (bsz, 1, d))
        if mixer == 0:
            wts, w_out = _mla_weights(mla_w_in[idx], mla_g_qa[idx], mla_w_qup[idx], mla_g_kva[idx],
                                      mla_w_kvup[idx], mla_w_out[idx])
            qt_l, k_l, vt_l, gt_l = _mla_pre(h_lat, g, lat(sc), lat(sh), wts, rope_lat, tm=512)
            qt_c, k_c, vt_c, gt_c = _mla_pre(h_ctx, g, cx(sc), cx(sh), wts, rope_ctx, tm=n_ctx)
            o_l = _attention(qt_l, [(k_c, vt_c), (k_l, vt_l)], qb=512, kb=512)
            h_new = _mla_post(o_l.reshape(bsz, d, n_tok), gt_l, w_out, lat(ga), h_lat, tm=512)
            if ctx_later:
                o_c = _attention(qt_c, [(k_c, vt_c)], qb=n_ctx, kb=n_ctx)
                h_ctx = _mla_post(o_c.reshape(bsz, d, n_ctx), gt_c, w_out, cx(ga), h_ctx, tm=n_ctx)
            h_lat = h_new
        else:
            w_in = fno_w_in[idx].astype(BF16)
            w_out = fno_w_out[idx].astype(BF16)
            wre, wim, sg = _fno_pre(h_lat, g, lat(sc), lat(sh), w_in, dft_lat["cs"], tm=512)
            zre, zim = _dft1(wre, wim, dft_lat["f1"], dft_lat["twc"], dft_lat["tws"], n_t1=4)
            h_new = _dft2_post(zre, zim, dft_lat["f2"], sg, w_out, lat(ga), h_lat, fg,
                               n1=dft_lat["n1"], kk=1, final=last)
            if ctx_later:
                wre, wim, sg = _fno_pre(h_ctx, g, cx(sc), cx(sh), w_in, dft_ctx["cs"], tm=n_ctx)
                h_ctx = _dft2_post(wre, wim, dft_ctx["f2"], sg, w_out, cx(ga), h_ctx, fg,
                                   n1=dft_ctx["n1"], kk=1, final=False)
            h_lat = h_new
    return h_lat
```

```python
import functools
import math

import numpy as np
import jax
import jax.numpy as jnp
from jax import lax
from jax.experimental import pallas as pl
from jax.experimental.pallas import tpu as pltpu

F32 = jnp.float32
BF16 = jnp.bfloat16

D_MODEL = 1024
DEPTH = 4
GRID_W = 64
NORM_EPS = 1e-6
HEADS = 8
QK_NOPE = 128
QK_ROPE = 64
V_DIM = 128
Q_LORA = 256
KV_LORA = 128
ROPE_THETA = 10000.0
ROPE_QUARTER = QK_ROPE // 4
QK_PAD = 256
FOURIER_GROUPS = 8
GROUP_DIM = D_MODEL // FOURIER_GROUPS
DFT_INNER = 128
VMEM_LIMIT = 56 * 1024 * 1024

NT_DIMS = (((1,), (1,)), ((), ()))
TN_DIMS = (((0,), (0,)), ((), ()))


def _params(*semantics):
    return pltpu.CompilerParams(dimension_semantics=semantics, vmem_limit_bytes=VMEM_LIMIT)


def _silu(x):
    return x * jax.nn.sigmoid(x)


def _rms(x):
    return x * lax.rsqrt(jnp.mean(x * x, axis=-1, keepdims=True) + NORM_EPS)


def _modulated_norm(h_ref, g_ref, sc_ref, sh_ref):
    u = (_rms(h_ref[0]) * g_ref[...]) * (1.0 + sc_ref[0]) + sh_ref[0]
    return u.astype(BF16)


def _ada_kernel(c_ref, w_ref, b_ref, o_ref):
    act = _silu(c_ref[...])
    o_ref[0] = jnp.dot(act, w_ref[0], precision=lax.Precision.HIGHEST,
                       preferred_element_type=F32) + b_ref[0]


def _ada(cond, w_ada, b_ada):
    depth, d, d3 = w_ada.shape
    rows = cond.shape[0]
    return pl.pallas_call(
        _ada_kernel,
        grid=(depth, d3 // d),
        in_specs=[
            pl.BlockSpec((rows, d), lambda i, j: (0, 0)),
            pl.BlockSpec((1, d, d), lambda i, j: (i, 0, j)),
            pl.BlockSpec((1, 1, d), lambda i, j: (i, 0, j)),
        ],
        out_specs=pl.BlockSpec((1, rows, d), lambda i, j: (i, 0, j)),
        out_shape=jax.ShapeDtypeStruct((depth, rows, d3), F32),
        compiler_params=_params("arbitrary", "arbitrary"),
        name="ada",
    )(cond, w_ada, b_ada.reshape(depth, 1, d3))


def _mla_pre_kernel(h_ref, g_ref, sc_ref, sh_ref, wa_ref, wgt_ref, gqa_ref, wqt_ref, gkva_ref,
                    wk_ref, wvt_ref, ropet_ref, cosk_ref, sink_ref,
                    qt_ref, k_ref, vt_ref, gt_ref, *, q_scale):
    ub = _modulated_norm(h_ref, g_ref, sc_ref, sh_ref)
    pa = jnp.dot(ub, wa_ref[...], preferred_element_type=F32)
    c_q = pa[:, :Q_LORA]
    c_kv = pa[:, Q_LORA:Q_LORA + KV_LORA]
    kpe = pa[:, 384:512]
    kpe_rot = pa[:, 512:640]

    gate_t = lax.dot_general(wgt_ref[...], ub, NT_DIMS, preferred_element_type=F32)
    gt_ref[0] = _silu(gate_t).astype(BF16)

    cqn = (_rms(c_q) * gqa_ref[...]).astype(BF16)
    q_t = lax.dot_general(wqt_ref[...], cqn, NT_DIMS, preferred_element_type=F32)
    rope_t = ropet_ref[...]
    r = ROPE_QUARTER
    cos_r, sin_r, cos_c, sin_c = (rope_t[i * r:(i + 1) * r] for i in range(4))
    zeros = jnp.zeros((QK_PAD - QK_NOPE - QK_ROPE, q_t.shape[1]), BF16)
    for hd in range(HEADS):
        b0 = hd * QK_PAD
        qt_ref[0, hd, 0:QK_NOPE, :] = (q_t[b0:b0 + QK_NOPE] * q_scale).astype(BF16)
        p0 = b0 + QK_NOPE
        a, b, c, d = (q_t[p0 + i * r:p0 + (i + 1) * r] for i in range(4))
        rot = jnp.concatenate([a * cos_r - b * sin_r, b * cos_r + a * sin_r,
                               c * cos_c - d * sin_c, d * cos_c + c * sin_c], axis=0)
        qt_ref[0, hd, QK_NOPE:QK_NOPE + QK_ROPE, :] = (rot * q_scale).astype(BF16)
        qt_ref[0, hd, QK_NOPE + QK_ROPE:QK_PAD, :] = zeros

    ckvn = (_rms(c_kv) * gkva_ref[...]).astype(BF16)
    k_nope = jnp.dot(ckvn, wk_ref[...], preferred_element_type=F32)
    v_t = lax.dot_general(wvt_ref[...], ckvn, NT_DIMS, preferred_element_type=F32)
    k_rope = (kpe * cosk_ref[...] + kpe_rot * sink_ref[...]).astype(BF16)
    for hd in range(HEADS):
        k_ref[0, hd, :, 0:QK_NOPE] = k_nope[:, hd * QK_NOPE:(hd + 1) * QK_NOPE].astype(BF16)
        k_ref[0, hd, :, QK_NOPE:QK_PAD] = k_rope
        vt_ref[0, hd] = v_t[hd * V_DIM:(hd + 1) * V_DIM].astype(BF16)


def _mla_pre(h, g, sc, sh, wts, tables, tm):
    bsz, t, d = h.shape
    wa, wgt, gqa, wqt, gkva, wk, wvt = wts
    rope_t, cosk, sink = tables
    q_scale = math.log2(math.e) / math.sqrt(QK_NOPE + QK_ROPE)
    const = lambda shape: pl.BlockSpec(shape, lambda b, i: (0,) * len(shape))
    vec = pl.BlockSpec((1, 1, d), lambda b, i: (b, 0, 0))
    return pl.pallas_call(
        functools.partial(_mla_pre_kernel, q_scale=q_scale),
        grid=(bsz, t // tm),
        in_specs=[
            pl.BlockSpec((1, tm, d), lambda b, i: (b, i, 0)),
            const((1, d)), vec, vec,
            const(wa.shape), const(wgt.shape), const(gqa.shape), const(wqt.shape),
            const(gkva.shape), const(wk.shape), const(wvt.shape),
            pl.BlockSpec((QK_ROPE, tm), lambda b, i: (0, i)),
            pl.BlockSpec((tm, 128), lambda b, i: (i, 0)),
            pl.BlockSpec((tm, 128), lambda b, i: (i, 0)),
        ],
        out_specs=[
            pl.BlockSpec((1, HEADS, QK_PAD, tm), lambda b, i: (b, 0, 0, i)),
            pl.BlockSpec((1, HEADS, tm, QK_PAD), lambda b, i: (b, 0, i, 0)),
            pl.BlockSpec((1, HEADS, V_DIM, tm), lambda b, i: (b, 0, 0, i)),
            pl.BlockSpec((1, HEADS * V_DIM, tm), lambda b, i: (b, 0, i)),
        ],
        out_shape=[
            jax.ShapeDtypeStruct((bsz, HEADS, QK_PAD, t), BF16),
            jax.ShapeDtypeStruct((bsz, HEADS, t, QK_PAD), BF16),
            jax.ShapeDtypeStruct((bsz, HEADS, V_DIM, t), BF16),
            jax.ShapeDtypeStruct((bsz, HEADS * V_DIM, t), BF16),
        ],
        compiler_params=_params("arbitrary", "arbitrary"),
        name="mla_pre",
    )(h, g, sc, sh, wa, wgt, gqa, wqt, gkva, wk, wvt, rope_t, cosk, sink)


def _attn_chunk(kc, vc, q_t, m_sc, l_sc, acc_sc, first):
    s = jnp.dot(kc, q_t, preferred_element_type=F32)
    mc = jnp.max(s, axis=0, keepdims=True)
    if first:
        m_new = mc
        p = jnp.exp2(s - m_new)
        l_sc[...] = jnp.sum(p, axis=0, keepdims=True)
        acc_sc[...] = jnp.dot(vc, p.astype(BF16), preferred_element_type=F32)
    else:
        m_old = m_sc[...]
        m_new = jnp.maximum(m_old, mc)
        alpha = jnp.exp2(m_old - m_new)
        p = jnp.exp2(s - m_new)
        l_sc[...] = alpha * l_sc[...] + jnp.sum(p, axis=0, keepdims=True)
        acc_sc[...] = alpha * acc_sc[...] + jnp.dot(vc, p.astype(BF16), preferred_element_type=F32)
    m_sc[...] = m_new


def _attn_kernel(*refs, n_src, kb):
    qt_ref = refs[0]
    srcs = refs[1:1 + 2 * n_src]
    o_ref = refs[1 + 2 * n_src]
    m_sc, l_sc, acc_sc = refs[2 + 2 * n_src:]
    q_t = qt_ref[0, 0]
    for s in range(n_src):
        k_ref, vt_ref = srcs[2 * s], srcs[2 * s + 1]
        tk = k_ref.shape[2]
        ck = min(kb, tk)
        n_chunks = tk // ck
        start = 0
        if s == 0:
            _attn_chunk(k_ref[0, 0, 0:ck, :], vt_ref[0, 0, :, 0:ck], q_t, m_sc, l_sc, acc_sc, True)
            start = 1
        if n_chunks > start:
            def body(i, carry, k_ref=k_ref, vt_ref=vt_ref, ck=ck):
                off = pl.multiple_of(i * ck, ck)
                _attn_chunk(k_ref[0, 0, pl.ds(off, ck), :], vt_ref[0, 0, :, pl.ds(off, ck)],
                            q_t, m_sc, l_sc, acc_sc, False)
                return carry
            lax.fori_loop(start, n_chunks, body, 0)
    o_ref[0, 0] = (acc_sc[...] / l_sc[...]).astype(o_ref.dtype)


def _attention(q_t, sources, qb, kb):
    bsz, heads, _, t = q_t.shape
    in_specs = [pl.BlockSpec((1, 1, QK_PAD, qb), lambda b, h, i: (b, h, 0, i))]
    args = [q_t]
    for k, v_t in sources:
        tk = k.shape[2]
        in_specs.append(pl.BlockSpec((1, 1, tk, QK_PAD), lambda b, h, i: (b, h, 0, 0)))
        in_specs.append(pl.BlockSpec((1, 1, V_DIM, tk), lambda b, h, i: (b, h, 0, 0)))
        args += [k, v_t]
    return pl.pallas_call(
        functools.partial(_attn_kernel, n_src=len(sources), kb=kb),
        grid=(bsz, heads, t // qb),
        in_specs=in_specs,
        out_specs=pl.BlockSpec((1, 1, V_DIM, qb), lambda b, h, i: (b, h, 0, i)),
        out_shape=jax.ShapeDtypeStruct((bsz, heads, V_DIM, t), BF16),
        scratch_shapes=[pltpu.VMEM((1, qb), F32), pltpu.VMEM((1, qb), F32), pltpu.VMEM((V_DIM, qb), F32)],
        compiler_params=_params("arbitrary", "arbitrary", "arbitrary"),
        name="attention",
    )(*args)


def _softmax_pv(s_ref, pv_fn, m_sc, l_sc, acc_sc, first):
    s = s_ref[...]
    mc = jnp.max(s, axis=0, keepdims=True)
    if first:
        m_new = mc
        p = jnp.exp2(s - m_new)
        l_sc[...] = jnp.sum(p, axis=0, keepdims=True)
        acc_sc[...] = pv_fn(p.astype(BF16))
    else:
        m_old = m_sc[...]
        m_new = jnp.maximum(m_old, mc)
        alpha = jnp.exp2(m_old - m_new)
        p = jnp.exp2(s - m_new)
        l_sc[...] = alpha * l_sc[...] + jnp.sum(p, axis=0, keepdims=True)
        acc_sc[...] = alpha * acc_sc[...] + pv_fn(p.astype(BF16))
    m_sc[...] = m_new


def _attn_lat_kernel(qt_ref, kc_ref, vtc_ref, kl_ref, vtl_ref, o_ref, s_a, s_b, m_sc, l_sc, acc_sc, *, kb):
    n_ctx = kc_ref.shape[2]
    n_lat = kl_ref.shape[2]
    n_chunks = (n_ctx + n_lat) // kb
    head = kb - n_ctx
    dot = functools.partial(jnp.dot, preferred_element_type=F32)

    def lat_offset(c):
        return pl.multiple_of(c * kb - n_ctx, math.gcd(kb, n_ctx))

    def scores(c, s_ref):
        q_t = qt_ref[0, 0]
        if isinstance(c, int) and c == 0:
            s_ref[0:n_ctx, :] = dot(kc_ref[0, 0], q_t)
            s_ref[n_ctx:kb, :] = dot(kl_ref[0, 0, 0:head, :], q_t)
        else:
            s_ref[...] = dot(kl_ref[0, 0, pl.ds(lat_offset(c), kb), :], q_t)

    def consume(c, s_ref):
        if isinstance(c, int) and c == 0:
            pv = lambda p: dot(vtc_ref[0, 0], p[0:n_ctx]) + dot(vtl_ref[0, 0, :, 0:head], p[n_ctx:kb])
            _softmax_pv(s_ref, pv, m_sc, l_sc, acc_sc, True)
        else:
            pv = lambda p: dot(vtl_ref[0, 0, :, pl.ds(lat_offset(c), kb)], p)
            _softmax_pv(s_ref, pv, m_sc, l_sc, acc_sc, False)

    scores(0, s_a)
    scores(1, s_b)
    consume(0, s_a)

    def pair(t, carry):
        c = 2 * t + 1
        scores(c + 1, s_a)
        consume(c, s_b)
        scores(c + 2, s_b)
        consume(c + 1, s_a)
        return carry
    lax.fori_loop(0, (n_chunks - 3) // 2, pair, 0)

    last = n_chunks - 1
    scores(last, s_a)
    consume(last - 1, s_b)
    consume(last, s_a)
    o_ref[0, 0] = (acc_sc[...] / l_sc[...]).astype(o_ref.dtype)


def _attention_lat(q_t, k_c, vt_c, k_l, vt_l, qb, kb):
    bsz, heads, _, t = q_t.shape
    n_ctx, n_lat = k_c.shape[2], k_l.shape[2]
    n_chunks = (n_ctx + n_lat) // kb
    assert n_chunks * kb == n_ctx + n_lat and n_chunks % 2 == 1 and n_chunks >= 3 and n_ctx < kb
    per_head = lambda shape: pl.BlockSpec((1, 1) + shape, lambda b, h, i: (b, h, 0, 0))
    return pl.pallas_call(
        functools.partial(_attn_lat_kernel, kb=kb),
        grid=(bsz, heads, t // qb),
        in_specs=[pl.BlockSpec((1, 1, QK_PAD, qb), lambda b, h, i: (b, h, 0, i)),
                  per_head((n_ctx, QK_PAD)), per_head((V_DIM, n_ctx)),
                  per_head((n_lat, QK_PAD)), per_head((V_DIM, n_lat))],
        out_specs=pl.BlockSpec((1, 1, V_DIM, qb), lambda b, h, i: (b, h, 0, i)),
        out_shape=jax.ShapeDtypeStruct((bsz, heads, V_DIM, t), BF16),
        scratch_shapes=[pltpu.VMEM((kb, qb), F32), pltpu.VMEM((kb, qb), F32),
                        pltpu.VMEM((1, qb), F32), pltpu.VMEM((1, qb), F32), pltpu.VMEM((V_DIM, qb), F32)],
        compiler_params=_params("arbitrary", "arbitrary", "arbitrary"),
        name="attention_lat",
    )(q_t, k_c, vt_c, k_l, vt_l)


def _mla_post_kernel(ot_ref, gt_ref, w_ref, ga_ref, h_ref, out_ref):
    og = (ot_ref[0].astype(F32) * gt_ref[0].astype(F32)).astype(BF16)
    y = lax.dot_general(og, w_ref[...], TN_DIMS, preferred_element_type=F32)
    out_ref[0] = h_ref[0] + ga_ref[0] * y


def _mla_post(o_t, g_t, w_out, ga, h, tm):
    bsz, t, d = h.shape
    return pl.pallas_call(
        _mla_post_kernel,
        grid=(bsz, t // tm),
        in_specs=[
            pl.BlockSpec((1, d, tm), lambda b, i: (b, 0, i)),
            pl.BlockSpec((1, d, tm), lambda b, i: (b, 0, i)),
            pl.BlockSpec((d, d), lambda b, i: (0, 0)),
            pl.BlockSpec((1, 1, d), lambda b, i: (b, 0, 0)),
            pl.BlockSpec((1, tm, d), lambda b, i: (b, i, 0)),
        ],
        out_specs=pl.BlockSpec((1, tm, d), lambda b, i: (b, i, 0)),
        out_shape=jax.ShapeDtypeStruct((bsz, t, d), F32),
        compiler_params=_params("arbitrary", "arbitrary"),
        name="mla_post",
    )(o_t, g_t, w_out, ga, h)


def _fno_pre_kernel(h_ref, g_ref, sc_ref, sh_ref, w_ref, cs_ref, wre_ref, wim_ref, sg_ref):
    ub = _modulated_norm(h_ref, g_ref, sc_ref, sh_ref)
    proj = jnp.dot(ub, w_ref[...], preferred_element_type=F32)
    d = sg_ref.shape[2]
    sg_ref[0] = _silu(proj[:, d:]).astype(BF16)
    zb = proj[:, :d].astype(BF16)
    for g in range(FOURIER_GROUPS):
        lo, hi = g * GROUP_DIM, (g + 1) * GROUP_DIM
        ab = jnp.dot(zb[:, lo:hi], cs_ref[...], preferred_element_type=F32)
        wre_ref[0, :, lo:hi] = ab[:, :GROUP_DIM].astype(BF16)
        wim_ref[0, :, lo:hi] = ab[:, GROUP_DIM:].astype(BF16)


def _fno_pre(h, g, sc, sh, w_in, cs, tm):
    bsz, t, d = h.shape
    vec = pl.BlockSpec((1, 1, d), lambda b, i: (b, 0, 0))
    tok = pl.BlockSpec((1, tm, d), lambda b, i: (b, i, 0))
    out = jax.ShapeDtypeStruct((bsz, t, d), BF16)
    return pl.pallas_call(
        _fno_pre_kernel,
        grid=(bsz, t // tm),
        in_specs=[tok, pl.BlockSpec((1, d), lambda b, i: (0, 0)), vec, vec,
                  pl.BlockSpec(w_in.shape, lambda b, i: (0, 0)),
                  pl.BlockSpec(cs.shape, lambda b, i: (0, 0))],
        out_specs=[tok, tok, tok],
        out_shape=[out, out, out],
        compiler_params=_params("arbitrary", "arbitrary"),
        name="fno_pre",
    )(h, g, sc, sh, w_in, cs)


def _dft1_kernel(xre_ref, xim_ref, f_ref, twc_ref, tws_ref, zre_ref, zim_ref, *, n_t1, cols):
    x = jnp.concatenate([xre_ref[0], xim_ref[0]], axis=0)
    z = jnp.dot(f_ref[...], x, preferred_element_type=F32)
    n2 = z.shape[0] // 2
    zr, zi = z[:n2], z[n2:]
    for j in range(n_t1):
        c, s = twc_ref[j], tws_ref[j]
        a, b = zr[:, j * cols:(j + 1) * cols], zi[:, j * cols:(j + 1) * cols]
        zre_ref[0, :, j * cols:(j + 1) * cols] = (a * c - b * s).astype(BF16)
        zim_ref[0, :, j * cols:(j + 1) * cols] = (a * s + b * c).astype(BF16)


def _dft1(wre, wim, f1, twc, tws, n_t1):
    bsz, t, d = wre.shape
    n2 = t // DFT_INNER
    xre = wre.reshape(bsz, n2, DFT_INNER * d)
    xim = wim.reshape(bsz, n2, DFT_INNER * d)
    tc = n_t1 * d
    blk = pl.BlockSpec((1, n2, tc), lambda b, i: (b, 0, i))
    tw = pl.BlockSpec((n_t1, n2, 1), lambda b, i: (i, 0, 0))
    out = jax.ShapeDtypeStruct(xre.shape, BF16)
    return pl.pallas_call(
        functools.partial(_dft1_kernel, n_t1=n_t1, cols=d),
        grid=(bsz, DFT_INNER // n_t1),
        in_specs=[blk, blk, pl.BlockSpec(f1.shape, lambda b, i: (0, 0)), tw, tw],
        out_specs=[blk, blk],
        out_shape=[out, out],
        compiler_params=_params("arbitrary", "arbitrary"),
        name="dft1",
    )(xre, xim, f1, twc, tws)


def _dft2_post_kernel(zre_ref, zim_ref, f_ref, sg_ref, w_ref, ga_ref, h_ref, fg_ref, out_ref,
                      *, kk, scale, final):
    d = w_ref.shape[0]
    for j in range(kk):
        zc = jnp.concatenate([zre_ref[0, j], zim_ref[0, j]], axis=0)
        y = jnp.dot(f_ref[...], zc, preferred_element_type=F32) * scale
        sg = sg_ref[0, :, j * d:(j + 1) * d].astype(F32)
        o = jnp.dot((y * sg).astype(BF16), w_ref[...], preferred_element_type=F32)
        hn = h_ref[0, :, j * d:(j + 1) * d] + ga_ref[0] * o
        if final:
            hn = _rms(hn) * fg_ref[...]
        out_ref[0, :, j * d:(j + 1) * d] = hn


def _dft2_post(zre, zim, f2, sg, w_out, ga, h, final_g, n1, kk, final):
    bsz, t, d = h.shape
    n2 = t // n1
    zre = zre.reshape(bsz, n2, n1, d)
    zim = zim.reshape(bsz, n2, n1, d)
    zblk = pl.BlockSpec((1, kk, n1, d), lambda b, i: (b, i, 0, 0))
    tok = pl.BlockSpec((1, n1, kk * d), lambda b, i: (b, 0, i))
    scale = 1.0 / math.sqrt(t * GROUP_DIM)
    out = pl.pallas_call(
        functools.partial(_dft2_post_kernel, kk=kk, scale=scale, final=final),
        grid=(bsz, n2 // kk),
        in_specs=[zblk, zblk, pl.BlockSpec(f2.shape, lambda b, i: (0, 0)), tok,
                  pl.BlockSpec((d, d), lambda b, i: (0, 0)),
                  pl.BlockSpec((1, 1, d), lambda b, i: (b, 0, 0)), tok,
                  pl.BlockSpec((1, d), lambda b, i: (0, 0))],
        out_specs=tok,
        out_shape=jax.ShapeDtypeStruct((bsz, n1, n2 * d), F32),
        compiler_params=_params("arbitrary", "arbitrary"),
        name="dft2_post",
    )(zre, zim, f2, sg.reshape(bsz, n1, n2 * d), w_out, ga, h.reshape(bsz, n1, n2 * d), final_g)
    return out.reshape(bsz, t, d)


def _dft_cos_sin(n):
    idx = np.arange(n)
    ang = 2.0 * np.pi * ((idx[:, None] * idx[None, :]) % n) / n
    return np.cos(ang), np.sin(ang)


def _dft_tables(t):
    cn, sn = _dft_cos_sin(GROUP_DIM)
    cs = np.concatenate([cn, sn], axis=1)
    if t <= 256:
        n1, n2 = t, 1
    else:
        n1, n2 = DFT_INNER, t // DFT_INNER
    c1, s1 = _dft_cos_sin(n1)
    f2 = np.concatenate([c1, -s1], axis=1)
    c2, s2 = _dft_cos_sin(n2)
    f1 = np.block([[c2, -s2], [s2, c2]])
    k2 = np.arange(n2)[None, :, None]
    t1 = np.arange(n1)[:, None, None]
    ang = 2.0 * np.pi * ((k2 * t1) % t) / t
    as_f32 = lambda a: jnp.asarray(a.astype(np.float32))
    return dict(n1=n1, n2=n2, cs=as_f32(cs).astype(BF16), f1=as_f32(f1).astype(BF16),
                f2=as_f32(f2).astype(BF16), twc=as_f32(np.cos(ang)), tws=as_f32(np.sin(ang)))


def _rope_tables(n_tok):
    inv_freq = 1.0 / (ROPE_THETA ** (jnp.arange(0, 2 * ROPE_QUARTER, 2, dtype=F32) / (2 * ROPE_QUARTER)))
    pos = jnp.arange(n_tok, dtype=jnp.int32)
    ang_r = (pos // GRID_W).astype(F32)[:, None] * inv_freq[None, :]
    ang_c = (pos % GRID_W).astype(F32)[:, None] * inv_freq[None, :]
    return jnp.cos(ang_r), jnp.sin(ang_r), jnp.cos(ang_c), jnp.sin(ang_c)


def _pack_rope(cos_r, sin_r, cos_c, sin_c):
    n_tok = cos_r.shape[0]
    pad = jnp.zeros((n_tok, 128 - QK_ROPE), F32)
    rope_t = jnp.concatenate([cos_r, sin_r, cos_c, sin_c], axis=1).T
    cosk = jnp.concatenate([cos_r, cos_r, cos_c, cos_c, pad], axis=1)
    sink = jnp.concatenate([sin_r, sin_r, sin_c, sin_c, pad], axis=1)
    return rope_t, cosk, sink


def _mla_weights(w_in, g_qa, w_qup, g_kva, w_kvup, w_out):
    d = w_in.shape[0]
    kv_hi = Q_LORA + KV_LORA
    pe_hi = kv_hi + QK_ROPE
    w_pe = w_in[:, kv_hi:pe_hi]
    r = ROPE_QUARTER
    w_rot = jnp.concatenate([-w_pe[:, r:2 * r], w_pe[:, 0:r], -w_pe[:, 3 * r:4 * r], w_pe[:, 2 * r:3 * r]], axis=1)
    zpad = jnp.zeros((d, 128 - QK_ROPE), w_in.dtype)
    wa = jnp.concatenate([w_in[:, :kv_hi], w_pe, zpad, w_rot, zpad], axis=1).astype(BF16)
    wgt = w_in[:, pe_hi:].T.astype(BF16)
    wq = w_qup.reshape(Q_LORA, HEADS, QK_NOPE + QK_ROPE)
    wq = jnp.pad(wq, ((0, 0), (0, 0), (0, QK_PAD - QK_NOPE - QK_ROPE)))
    wqt = wq.reshape(Q_LORA, HEADS * QK_PAD).T.astype(BF16)
    wkv = w_kvup.reshape(KV_LORA, HEADS, QK_NOPE + V_DIM)
    wk = wkv[:, :, :QK_NOPE].reshape(KV_LORA, HEADS * QK_NOPE).astype(BF16)
    wvt = wkv[:, :, QK_NOPE:].reshape(KV_LORA, HEADS * V_DIM).T.astype(BF16)
    return (wa, wgt, g_qa.reshape(1, Q_LORA), wqt, g_kva.reshape(1, KV_LORA), wk, wvt), w_out.astype(BF16)


def kernel(x, c, ctx, c_ctx, norm_g, w_ada, b_ada, mla_w_in, mla_g_qa, mla_w_qup, mla_g_kva,
           mla_w_kvup, mla_w_out, fno_w_in, fno_w_out, final_g):
    bsz, n_tok, d = x.shape
    n_ctx = ctx.shape[1]
    depth = norm_g.shape[0]

    rows = 8
    cond = jnp.concatenate([c, c_ctx[None, :], jnp.zeros((rows - bsz - 1, d), F32)], axis=0)
    mods = _ada(cond, w_ada, b_ada)

    rope_lat = _pack_rope(*_rope_tables(n_tok))
    ones, zero = jnp.ones((n_ctx, ROPE_QUARTER), F32), jnp.zeros((n_ctx, ROPE_QUARTER), F32)
    rope_ctx = _pack_rope(ones, zero, ones, zero)
    dft_lat = _dft_tables(n_tok)
    dft_ctx = _dft_tables(n_ctx)
    fg = final_g.reshape(1, d)

    h_lat, h_ctx = x, ctx
    for i in range(depth):
        mixer, idx = i % 2, i // 2
        ctx_later = any(j % 2 == 0 for j in range(i + 1, depth))
        last = i == depth - 1
        g = norm_g[i].reshape(1, d)
        sh, sc, ga = (mods[i, :, j * d:(j + 1) * d] for j in range(3))
        lat = lambda a: a[:bsz].reshape(bsz, 1, d)
        cx = lambda a: jnp.broadcast_to(a[bsz].reshape(1, 1, d), (bsz, 1, d))
        if mixer == 0:
            wts, w_out = _mla_weights(mla_w_in[idx], mla_g_qa[idx], mla_w_qup[idx], mla_g_kva[idx],
                                      mla_w_kvup[idx], mla_w_out[idx])
            qt_l, k_l, vt_l, gt_l = _mla_pre(h_lat, g, lat(sc), lat(sh), wts, rope_lat, tm=512)
            qt_c, k_c, vt_c, gt_c = _mla_pre(h_ctx, g, cx(sc), cx(sh), wts, rope_ctx, tm=n_ctx)
            o_l = _attention_lat(qt_l, k_c, vt_c, k_l, vt_l, qb=512, kb=768)
            h_new = _mla_post(o_l.reshape(bsz, d, n_tok), gt_l, w_out, lat(ga), h_lat, tm=512)
            if ctx_later:
                o_c = _attention(qt_c, [(k_c, vt_c)], qb=n_ctx, kb=n_ctx)
                h_ctx = _mla_post(o_c.reshape(bsz, d, n_ctx), gt_c, w_out, cx(ga), h_ctx, tm=n_ctx)
            h_lat = h_new
        else:
            w_in = fno_w_in[idx].astype(BF16)
            w_out = fno_w_out[idx].astype(BF16)
            wre, wim, sg = _fno_pre(h_lat, g, lat(sc), lat(sh), w_in, dft_lat["cs"], tm=512)
            zre, zim = _dft1(wre, wim, dft_lat["f1"], dft_lat["twc"], dft_lat["tws"], n_t1=4)
            h_new = _dft2_post(zre, zim, dft_lat["f2"], sg, w_out, lat(ga), h_lat, fg,
                               n1=dft_lat["n1"], kk=1, final=last)
            if ctx_later:
                wre, wim, sg = _fno_pre(h_ctx, g, cx(sc), cx(sh), w_in, dft_ctx["cs"], tm=n_ctx)
                h_ctx = _dft2_post(wre, wim, dft_ctx["f2"], sg, w_out, cx(ga), h_ctx, fg,
                                   n1=dft_ctx["n1"], kk=1, final=False)
            h_lat = h_new
    return h_lat
```

```python
import functools
import math

import numpy as np
import jax
import jax.numpy as jnp
from jax import lax
from jax.experimental import pallas as pl
from jax.experimental.pallas import tpu as pltpu

F32 = jnp.float32
BF16 = jnp.bfloat16

D_MODEL = 1024
DEPTH = 4
GRID_W = 64
NORM_EPS = 1e-6
HEADS = 8
QK_NOPE = 128
QK_ROPE = 64
V_DIM = 128
Q_LORA = 256
KV_LORA = 128
ROPE_THETA = 10000.0
ROPE_QUARTER = QK_ROPE // 4
QK_PAD = 256
FOURIER_GROUPS = 8
GROUP_DIM = D_MODEL // FOURIER_GROUPS
DFT_INNER = 128
VMEM_LIMIT = 56 * 1024 * 1024

NT_DIMS = (((1,), (1,)), ((), ()))
TN_DIMS = (((0,), (0,)), ((), ()))


def _params(*semantics):
    return pltpu.CompilerParams(dimension_semantics=semantics, vmem_limit_bytes=VMEM_LIMIT)


def _silu(x):
    return x * jax.nn.sigmoid(x)


def _rms(x):
    return x * lax.rsqrt(jnp.mean(x * x, axis=-1, keepdims=True) + NORM_EPS)


def _modulated_norm(h_ref, g_ref, sc_ref, sh_ref):
    u = (_rms(h_ref[0]) * g_ref[...]) * (1.0 + sc_ref[0]) + sh_ref[0]
    return u.astype(BF16)


def _split_bf16(x):
    hi = x.astype(BF16)
    return hi, (x - hi.astype(F32)).astype(BF16)


def _ada_kernel(c_ref, w_ref, b_ref, o_ref):
    a_hi, a_lo = _split_bf16(_silu(c_ref[...]))
    w_hi, w_lo = _split_bf16(w_ref[0])
    dot = functools.partial(jnp.dot, preferred_element_type=F32)
    o_ref[0] = dot(a_hi, w_hi) + (dot(a_hi, w_lo) + dot(a_lo, w_hi)) + b_ref[0]


def _ada(cond, w_ada, b_ada):
    depth, d, d3 = w_ada.shape
    rows = cond.shape[0]
    return pl.pallas_call(
        _ada_kernel,
        grid=(depth, d3 // d),
        in_specs=[
            pl.BlockSpec((rows, d), lambda i, j: (0, 0)),
            pl.BlockSpec((1, d, d), lambda i, j: (i, 0, j)),
            pl.BlockSpec((1, 1, d), lambda i, j: (i, 0, j)),
        ],
        out_specs=pl.BlockSpec((1, rows, d), lambda i, j: (i, 0, j)),
        out_shape=jax.ShapeDtypeStruct((depth, rows, d3), F32),
        compiler_params=_params("arbitrary", "arbitrary"),
        name="ada",
    )(cond, w_ada, b_ada.reshape(depth, 1, d3))


def _mla_pre_kernel(h_ref, g_ref, sc_ref, sh_ref, wa_ref, wgt_ref, gqa_ref, wqt_ref, gkva_ref,
                    wk_ref, wvt_ref, ropet_ref, cosk_ref, sink_ref,
                    qt_ref, k_ref, vt_ref, gt_ref, *, q_scale):
    ub = _modulated_norm(h_ref, g_ref, sc_ref, sh_ref)
    pa = jnp.dot(ub, wa_ref[...], preferred_element_type=F32)
    c_q = pa[:, :Q_LORA]
    c_kv = pa[:, Q_LORA:Q_LORA + KV_LORA]
    kpe = pa[:, 384:512]
    kpe_rot = pa[:, 512:640]

    gate_t = lax.dot_general(wgt_ref[...], ub, NT_DIMS, preferred_element_type=F32)
    gt_ref[0] = _silu(gate_t).astype(BF16)

    cqn = (_rms(c_q) * gqa_ref[...]).astype(BF16)
    q_t = lax.dot_general(wqt_ref[...], cqn, NT_DIMS, preferred_element_type=F32)
    rope_t = ropet_ref[...]
    r = ROPE_QUARTER
    cos_r, sin_r, cos_c, sin_c = (rope_t[i * r:(i + 1) * r] for i in range(4))
    zeros = jnp.zeros((QK_PAD - QK_NOPE - QK_ROPE, q_t.shape[1]), BF16)
    for hd in range(HEADS):
        b0 = hd * QK_PAD
        qt_ref[0, hd, 0:QK_NOPE, :] = (q_t[b0:b0 + QK_NOPE] * q_scale).astype(BF16)
        p0 = b0 + QK_NOPE
        a, b, c, d = (q_t[p0 + i * r:p0 + (i + 1) * r] for i in range(4))
        rot = jnp.concatenate([a * cos_r - b * sin_r, b * cos_r + a * sin_r,
                               c * cos_c - d * sin_c, d * cos_c + c * sin_c], axis=0)
        qt_ref[0, hd, QK_NOPE:QK_NOPE + QK_ROPE, :] = (rot * q_scale).astype(BF16)
        qt_ref[0, hd, QK_NOPE + QK_ROPE:QK_PAD, :] = zeros

    ckvn = (_rms(c_kv) * gkva_ref[...]).astype(BF16)
    k_nope = jnp.dot(ckvn, wk_ref[...], preferred_element_type=F32)
    v_t = lax.dot_general(wvt_ref[...], ckvn, NT_DIMS, preferred_element_type=F32)
    k_rope = (kpe * cosk_ref[...] + kpe_rot * sink_ref[...]).astype(BF16)
    for hd in range(HEADS):
        k_ref[0, hd, :, 0:QK_NOPE] = k_nope[:, hd * QK_NOPE:(hd + 1) * QK_NOPE].astype(BF16)
        k_ref[0, hd, :, QK_NOPE:QK_PAD] = k_rope
        vt_ref[0, hd] = v_t[hd * V_DIM:(hd + 1) * V_DIM].astype(BF16)


def _mla_pre(h, g, sc, sh, wts, tables, tm):
    bsz, t, d = h.shape
    wa, wgt, gqa, wqt, gkva, wk, wvt = wts
    rope_t, cosk, sink = tables
    q_scale = math.log2(math.e) / math.sqrt(QK_NOPE + QK_ROPE)
    const = lambda shape: pl.BlockSpec(shape, lambda b, i: (0,) * len(shape))
    vec = pl.BlockSpec((1, 1, d), lambda b, i: (b, 0, 0))
    return pl.pallas_call(
        functools.partial(_mla_pre_kernel, q_scale=q_scale),
        grid=(bsz, t // tm),
        in_specs=[
            pl.BlockSpec((1, tm, d), lambda b, i: (b, i, 0)),
            const((1, d)), vec, vec,
            const(wa.shape), const(wgt.shape), const(gqa.shape), const(wqt.shape),
            const(gkva.shape), const(wk.shape), const(wvt.shape),
            pl.BlockSpec((QK_ROPE, tm), lambda b, i: (0, i)),
            pl.BlockSpec((tm, 128), lambda b, i: (i, 0)),
            pl.BlockSpec((tm, 128), lambda b, i: (i, 0)),
        ],
        out_specs=[
            pl.BlockSpec((1, HEADS, QK_PAD, tm), lambda b, i: (b, 0, 0, i)),
            pl.BlockSpec((1, HEADS, tm, QK_PAD), lambda b, i: (b, 0, i, 0)),
            pl.BlockSpec((1, HEADS, V_DIM, tm), lambda b, i: (b, 0, 0, i)),
            pl.BlockSpec((1, HEADS * V_DIM, tm), lambda b, i: (b, 0, i)),
        ],
        out_shape=[
            jax.ShapeDtypeStruct((bsz, HEADS, QK_PAD, t), BF16),
            jax.ShapeDtypeStruct((bsz, HEADS, t, QK_PAD), BF16),
            jax.ShapeDtypeStruct((bsz, HEADS, V_DIM, t), BF16),
            jax.ShapeDtypeStruct((bsz, HEADS * V_DIM, t), BF16),
        ],
        compiler_params=_params("arbitrary", "arbitrary"),
        name="mla_pre",
    )(h, g, sc, sh, wa, wgt, gqa, wqt, gkva, wk, wvt, rope_t, cosk, sink)


def _attn_chunk(kc, vc, q_t, m_sc, l_sc, acc_sc, first):
    s = jnp.dot(kc, q_t, preferred_element_type=F32)
    mc = jnp.max(s, axis=0, keepdims=True)
    if first:
        m_new = mc
        p = jnp.exp2(s - m_new)
        l_sc[...] = jnp.sum(p, axis=0, keepdims=True)
        acc_sc[...] = jnp.dot(vc, p.astype(BF16), preferred_element_type=F32)
    else:
        m_old = m_sc[...]
        m_new = jnp.maximum(m_old, mc)
        alpha = jnp.exp2(m_old - m_new)
        p = jnp.exp2(s - m_new)
        l_sc[...] = alpha * l_sc[...] + jnp.sum(p, axis=0, keepdims=True)
        acc_sc[...] = alpha * acc_sc[...] + jnp.dot(vc, p.astype(BF16), preferred_element_type=F32)
    m_sc[...] = m_new


def _attn_kernel(*refs, n_src, kb):
    qt_ref = refs[0]
    srcs = refs[1:1 + 2 * n_src]
    o_ref = refs[1 + 2 * n_src]
    m_sc, l_sc, acc_sc = refs[2 + 2 * n_src:]
    q_t = qt_ref[0, 0]
    for s in range(n_src):
        k_ref, vt_ref = srcs[2 * s], srcs[2 * s + 1]
        tk = k_ref.shape[2]
        ck = min(kb, tk)
        n_chunks = tk // ck
        start = 0
        if s == 0:
            _attn_chunk(k_ref[0, 0, 0:ck, :], vt_ref[0, 0, :, 0:ck], q_t, m_sc, l_sc, acc_sc, True)
            start = 1
        if n_chunks > start:
            def body(i, carry, k_ref=k_ref, vt_ref=vt_ref, ck=ck):
                off = pl.multiple_of(i * ck, ck)
                _attn_chunk(k_ref[0, 0, pl.ds(off, ck), :], vt_ref[0, 0, :, pl.ds(off, ck)],
                            q_t, m_sc, l_sc, acc_sc, False)
                return carry
            lax.fori_loop(start, n_chunks, body, 0)
    o_ref[0, 0] = (acc_sc[...] / l_sc[...]).astype(o_ref.dtype)


def _attention(q_t, sources, qb, kb):
    bsz, heads, _, t = q_t.shape
    in_specs = [pl.BlockSpec((1, 1, QK_PAD, qb), lambda b, h, i: (b, h, 0, i))]
    args = [q_t]
    for k, v_t in sources:
        tk = k.shape[2]
        in_specs.append(pl.BlockSpec((1, 1, tk, QK_PAD), lambda b, h, i: (b, h, 0, 0)))
        in_specs.append(pl.BlockSpec((1, 1, V_DIM, tk), lambda b, h, i: (b, h, 0, 0)))
        args += [k, v_t]
    return pl.pallas_call(
        functools.partial(_attn_kernel, n_src=len(sources), kb=kb),
        grid=(bsz, heads, t // qb),
        in_specs=in_specs,
        out_specs=pl.BlockSpec((1, 1, V_DIM, qb), lambda b, h, i: (b, h, 0, i)),
        out_shape=jax.ShapeDtypeStruct((bsz, heads, V_DIM, t), BF16),
        scratch_shapes=[pltpu.VMEM((1, qb), F32), pltpu.VMEM((1, qb), F32), pltpu.VMEM((V_DIM, qb), F32)],
        compiler_params=_params("arbitrary", "arbitrary", "arbitrary"),
        name="attention",
    )(*args)


def _softmax_pv(s_ref, pv_fn, m_sc, l_sc, acc_sc, first):
    s = s_ref[...]
    mc = jnp.max(s, axis=0, keepdims=True)
    if first:
        m_new = mc
        p = jnp.exp2(s - m_new)
        l_sc[...] = jnp.sum(p, axis=0, keepdims=True)
        acc_sc[...] = pv_fn(p.astype(BF16))
    else:
        m_old = m_sc[...]
        m_new = jnp.maximum(m_old, mc)
        alpha = jnp.exp2(m_old - m_new)
        p = jnp.exp2(s - m_new)
        l_sc[...] = alpha * l_sc[...] + jnp.sum(p, axis=0, keepdims=True)
        acc_sc[...] = alpha * acc_sc[...] + pv_fn(p.astype(BF16))
    m_sc[...] = m_new


def _attn_lat_kernel(qt_ref, kc_ref, vtc_ref, kl_ref, vtl_ref, o_ref, s_a, s_b, m_sc, l_sc, acc_sc, *, kb):
    n_ctx = kc_ref.shape[2]
    n_lat = kl_ref.shape[2]
    n_chunks = (n_ctx + n_lat) // kb
    head = kb - n_ctx
    dot = functools.partial(jnp.dot, preferred_element_type=F32)

    def lat_offset(c):
        return pl.multiple_of(c * kb - n_ctx, math.gcd(kb, n_ctx))

    def scores(c, s_ref):
        q_t = qt_ref[0, 0]
        if isinstance(c, int) and c == 0:
            s_ref[0:n_ctx, :] = dot(kc_ref[0, 0], q_t)
            s_ref[n_ctx:kb, :] = dot(kl_ref[0, 0, 0:head, :], q_t)
        else:
            s_ref[...] = dot(kl_ref[0, 0, pl.ds(lat_offset(c), kb), :], q_t)

    def consume(c, s_ref):
        if isinstance(c, int) and c == 0:
            pv = lambda p: dot(vtc_ref[0, 0], p[0:n_ctx]) + dot(vtl_ref[0, 0, :, 0:head], p[n_ctx:kb])
            _softmax_pv(s_ref, pv, m_sc, l_sc, acc_sc, True)
        else:
            pv = lambda p: dot(vtl_ref[0, 0, :, pl.ds(lat_offset(c), kb)], p)
            _softmax_pv(s_ref, pv, m_sc, l_sc, acc_sc, False)

    scores(0, s_a)
    scores(1, s_b)
    consume(0, s_a)

    def quad(t, carry):
        c = 4 * t + 1
        for j in range(0, 4, 2):
            scores(c + j + 1, s_a)
            consume(c + j, s_b)
            scores(c + j + 2, s_b)
            consume(c + j + 1, s_a)
        return carry
    lax.fori_loop(0, (n_chunks - 3) // 4, quad, 0)

    last = n_chunks - 1
    scores(last, s_a)
    consume(last - 1, s_b)
    consume(last, s_a)
    o_ref[0, 0] = (acc_sc[...] / l_sc[...]).astype(o_ref.dtype)


def _attention_lat(q_t, k_c, vt_c, k_l, vt_l, qb, kb):
    bsz, heads, _, t = q_t.shape
    n_ctx, n_lat = k_c.shape[2], k_l.shape[2]
    n_chunks = (n_ctx + n_lat) // kb
    assert n_chunks * kb == n_ctx + n_lat and (n_chunks - 3) % 4 == 0 and n_ctx < kb
    per_head = lambda shape: pl.BlockSpec((1, 1) + shape, lambda b, h, i: (b, h, 0, 0))
    return pl.pallas_call(
        functools.partial(_attn_lat_kernel, kb=kb),
        grid=(bsz, heads, t // qb),
        in_specs=[pl.BlockSpec((1, 1, QK_PAD, qb), lambda b, h, i: (b, h, 0, i)),
                  per_head((n_ctx, QK_PAD)), per_head((V_DIM, n_ctx)),
                  per_head((n_lat, QK_PAD)), per_head((V_DIM, n_lat))],
        out_specs=pl.BlockSpec((1, 1, V_DIM, qb), lambda b, h, i: (b, h, 0, i)),
        out_shape=jax.ShapeDtypeStruct((bsz, heads, V_DIM, t), BF16),
        scratch_shapes=[pltpu.VMEM((kb, qb), F32), pltpu.VMEM((kb, qb), F32),
                        pltpu.VMEM((1, qb), F32), pltpu.VMEM((1, qb), F32), pltpu.VMEM((V_DIM, qb), F32)],
        compiler_params=_params("arbitrary", "arbitrary", "arbitrary"),
        name="attention_lat",
    )(q_t, k_c, vt_c, k_l, vt_l)


def _mla_post_kernel(ot_ref, gt_ref, w_ref, ga_ref, h_ref, out_ref):
    og = (ot_ref[0].astype(F32) * gt_ref[0].astype(F32)).astype(BF16)
    y = lax.dot_general(og, w_ref[...], TN_DIMS, preferred_element_type=F32)
    out_ref[0] = h_ref[0] + ga_ref[0] * y


def _mla_post(o_t, g_t, w_out, ga, h, tm):
    bsz, t, d = h.shape
    return pl.pallas_call(
        _mla_post_kernel,
        grid=(bsz, t // tm),
        in_specs=[
            pl.BlockSpec((1, d, tm), lambda b, i: (b, 0, i)),
            pl.BlockSpec((1, d, tm), lambda b, i: (b, 0, i)),
            pl.BlockSpec((d, d), lambda b, i: (0, 0)),
            pl.BlockSpec((1, 1, d), lambda b, i: (b, 0, 0)),
            pl.BlockSpec((1, tm, d), lambda b, i: (b, i, 0)),
        ],
        out_specs=pl.BlockSpec((1, tm, d), lambda b, i: (b, i, 0)),
        out_shape=jax.ShapeDtypeStruct((bsz, t, d), F32),
        compiler_params=_params("arbitrary", "arbitrary"),
        name="mla_post",
    )(o_t, g_t, w_out, ga, h)


def _fno_pre_kernel(h_ref, g_ref, sc_ref, sh_ref, w_ref, cs_ref, wre_ref, wim_ref, sg_ref):
    ub = _modulated_norm(h_ref, g_ref, sc_ref, sh_ref)
    proj = jnp.dot(ub, w_ref[...], preferred_element_type=F32)
    d = sg_ref.shape[2]
    sg_ref[0] = _silu(proj[:, d:]).astype(BF16)
    zb = proj[:, :d].astype(BF16)
    for g in range(FOURIER_GROUPS):
        lo, hi = g * GROUP_DIM, (g + 1) * GROUP_DIM
        ab = jnp.dot(zb[:, lo:hi], cs_ref[...], preferred_element_type=F32)
        wre_ref[0, :, lo:hi] = ab[:, :GROUP_DIM].astype(BF16)
        wim_ref[0, :, lo:hi] = ab[:, GROUP_DIM:].astype(BF16)


def _fno_pre(h, g, sc, sh, w_in, cs, tm):
    bsz, t, d = h.shape
    vec = pl.BlockSpec((1, 1, d), lambda b, i: (b, 0, 0))
    tok = pl.BlockSpec((1, tm, d), lambda b, i: (b, i, 0))
    out = jax.ShapeDtypeStruct((bsz, t, d), BF16)
    return pl.pallas_call(
        _fno_pre_kernel,
        grid=(bsz, t // tm),
        in_specs=[tok, pl.BlockSpec((1, d), lambda b, i: (0, 0)), vec, vec,
                  pl.BlockSpec(w_in.shape, lambda b, i: (0, 0)),
                  pl.BlockSpec(cs.shape, lambda b, i: (0, 0))],
        out_specs=[tok, tok, tok],
        out_shape=[out, out, out],
        compiler_params=_params("arbitrary", "arbitrary"),
        name="fno_pre",
    )(h, g, sc, sh, w_in, cs)


def _dft1_kernel(xre_ref, xim_ref, f_ref, twc_ref, tws_ref, zre_ref, zim_ref):
    xre = jnp.swapaxes(xre_ref[0], 0, 1)
    xim = jnp.swapaxes(xim_ref[0], 0, 1)
    n2 = xre.shape[1]
    for j in range(xre.shape[0]):
        x = jnp.concatenate([xre[j], xim[j]], axis=0)
        z = jnp.dot(f_ref[...], x, preferred_element_type=F32)
        zr, zi = z[:n2], z[n2:]
        c, s = twc_ref[j], tws_ref[j]
        zre_ref[0, j] = (zr * c - zi * s).astype(BF16)
        zim_ref[0, j] = (zr * s + zi * c).astype(BF16)


def _dft1(wre, wim, f1, twc, tws, g1, cols):
    bsz, t, d = wre.shape
    n2 = t // DFT_INNER
    assert n2 * DFT_INNER == t and DFT_INNER % g1 == 0 and d % cols == 0
    xre = wre.reshape(bsz, n2, DFT_INNER, d)
    xim = wim.reshape(bsz, n2, DFT_INNER, d)
    blk_in = pl.BlockSpec((1, n2, g1, cols), lambda b, i, j: (b, 0, i, j))
    blk_out = pl.BlockSpec((1, g1, n2, cols), lambda b, i, j: (b, i, 0, j))
    tw = pl.BlockSpec((g1, n2, 1), lambda b, i, j: (i, 0, 0))
    out = jax.ShapeDtypeStruct((bsz, DFT_INNER, n2, d), BF16)
    return pl.pallas_call(
        _dft1_kernel,
        grid=(bsz, DFT_INNER // g1, d // cols),
        in_specs=[blk_in, blk_in, pl.BlockSpec(f1.shape, lambda b, i, j: (0, 0)), tw, tw],
        out_specs=[blk_out, blk_out],
        out_shape=[out, out],
        compiler_params=_params("arbitrary", "arbitrary", "arbitrary"),
        name="dft1",
    )(xre, xim, f1, twc, tws)


def _dft2_kernel(zre_ref, zim_ref, f_ref, y_ref, *, scale):
    zre = jnp.swapaxes(zre_ref[0], 0, 1)
    zim = jnp.swapaxes(zim_ref[0], 0, 1)
    ys = []
    for j in range(zre.shape[0]):
        zc = jnp.concatenate([zre[j], zim[j]], axis=0)
        ys.append(jnp.dot(f_ref[...], zc, preferred_element_type=F32) * scale)
    y_ref[0] = jnp.swapaxes(jnp.stack(ys, axis=0), 0, 1).astype(y_ref.dtype)


def _dft2(zre, zim, f2, g2, cols):
    bsz, n1, n2, d = zre.shape
    assert n2 % g2 == 0 and d % cols == 0
    scale = 1.0 / math.sqrt(n1 * n2 * GROUP_DIM)
    blk = pl.BlockSpec((1, n1, g2, cols), lambda b, i, j: (b, 0, i, j))
    y = pl.pallas_call(
        functools.partial(_dft2_kernel, scale=scale),
        grid=(bsz, n2 // g2, d // cols),
        in_specs=[blk, blk, pl.BlockSpec(f2.shape, lambda b, i, j: (0, 0))],
        out_specs=blk,
        out_shape=jax.ShapeDtypeStruct((bsz, n1, n2, d), BF16),
        compiler_params=_params("arbitrary", "arbitrary", "arbitrary"),
        name="dft2",
    )(zre, zim, f2)
    return y.reshape(bsz, n1 * n2, d)


def _fno_post_kernel(y_ref, sg_ref, w_ref, ga_ref, h_ref, fg_ref, out_ref, *, final):
    yg = (y_ref[0].astype(F32) * sg_ref[0].astype(F32)).astype(BF16)
    hn = h_ref[0] + ga_ref[0] * jnp.dot(yg, w_ref[...], preferred_element_type=F32)
    if final:
        hn = _rms(hn) * fg_ref[...]
    out_ref[0] = hn


def _fno_post(y, sg, w_out, ga, h, final_g, tm, final):
    bsz, t, d = h.shape
    tok = pl.BlockSpec((1, tm, d), lambda b, i: (b, i, 0))
    return pl.pallas_call(
        functools.partial(_fno_post_kernel, final=final),
        grid=(bsz, t // tm),
        in_specs=[tok, tok, pl.BlockSpec((d, d), lambda b, i: (0, 0)),
                  pl.BlockSpec((1, 1, d), lambda b, i: (b, 0, 0)), tok,
                  pl.BlockSpec((1, d), lambda b, i: (0, 0))],
        out_specs=tok,
        out_shape=jax.ShapeDtypeStruct((bsz, t, d), F32),
        compiler_params=_params("arbitrary", "arbitrary"),
        name="fno_post",
    )(y, sg, w_out, ga, h, final_g)


def _dft2_post_kernel(zre_ref, zim_ref, f_ref, sg_ref, w_ref, ga_ref, h_ref, fg_ref, out_ref,
                      *, kk, scale, final):
    d = w_ref.shape[0]
    for j in range(kk):
        zc = jnp.concatenate([zre_ref[0, j], zim_ref[0, j]], axis=0)
        y = jnp.dot(f_ref[...], zc, preferred_element_type=F32) * scale
        sg = sg_ref[0, :, j * d:(j + 1) * d].astype(F32)
        o = jnp.dot((y * sg).astype(BF16), w_ref[...], preferred_element_type=F32)
        hn = h_ref[0, :, j * d:(j + 1) * d] + ga_ref[0] * o
        if final:
            hn = _rms(hn) * fg_ref[...]
        out_ref[0, :, j * d:(j + 1) * d] = hn


def _dft2_post(zre, zim, f2, sg, w_out, ga, h, final_g, n1, kk, final):
    bsz, t, d = h.shape
    n2 = t // n1
    zre = zre.reshape(bsz, n2, n1, d)
    zim = zim.reshape(bsz, n2, n1, d)
    zblk = pl.BlockSpec((1, kk, n1, d), lambda b, i: (b, i, 0, 0))
    tok = pl.BlockSpec((1, n1, kk * d), lambda b, i: (b, 0, i))
    scale = 1.0 / math.sqrt(t * GROUP_DIM)
    out = pl.pallas_call(
        functools.partial(_dft2_post_kernel, kk=kk, scale=scale, final=final),
        grid=(bsz, n2 // kk),
        in_specs=[zblk, zblk, pl.BlockSpec(f2.shape, lambda b, i: (0, 0)), tok,
                  pl.BlockSpec((d, d), lambda b, i: (0, 0)),
                  pl.BlockSpec((1, 1, d), lambda b, i: (b, 0, 0)), tok,
                  pl.BlockSpec((1, d), lambda b, i: (0, 0))],
        out_specs=tok,
        out_shape=jax.ShapeDtypeStruct((bsz, n1, n2 * d), F32),
        compiler_params=_params("arbitrary", "arbitrary"),
        name="dft2_post",
    )(zre, zim, f2, sg.reshape(bsz, n1, n2 * d), w_out, ga, h.reshape(bsz, n1, n2 * d), final_g)
    return out.reshape(bsz, t, d)


def _dft_cos_sin(n):
    idx = np.arange(n)
    ang = 2.0 * np.pi * ((idx[:, None] * idx[None, :]) % n) / n
    return np.cos(ang), np.sin(ang)


def _dft_tables(t):
    cn, sn = _dft_cos_sin(GROUP_DIM)
    cs = np.concatenate([cn, sn], axis=1)
    if t <= 256:
        n1, n2 = t, 1
    else:
        n1, n2 = DFT_INNER, t // DFT_INNER
    c1, s1 = _dft_cos_sin(n1)
    f2 = np.concatenate([c1, -s1], axis=1)
    c2, s2 = _dft_cos_sin(n2)
    f1 = np.block([[c2, -s2], [s2, c2]])
    k2 = np.arange(n2)[None, :, None]
    t1 = np.arange(n1)[:, None, None]
    ang = 2.0 * np.pi * ((k2 * t1) % t) / t
    as_f32 = lambda a: jnp.asarray(a.astype(np.float32))
    return dict(n1=n1, n2=n2, cs=as_f32(cs).astype(BF16), f1=as_f32(f1).astype(BF16),
                f2=as_f32(f2).astype(BF16), twc=as_f32(np.cos(ang)), tws=as_f32(np.sin(ang)))


def _rope_tables(n_tok):
    inv_freq = 1.0 / (ROPE_THETA ** (jnp.arange(0, 2 * ROPE_QUARTER, 2, dtype=F32) / (2 * ROPE_QUARTER)))
    pos = jnp.arange(n_tok, dtype=jnp.int32)
    ang_r = (pos // GRID_W).astype(F32)[:, None] * inv_freq[None, :]
    ang_c = (pos % GRID_W).astype(F32)[:, None] * inv_freq[None, :]
    return jnp.cos(ang_r), jnp.sin(ang_r), jnp.cos(ang_c), jnp.sin(ang_c)


def _pack_rope(cos_r, sin_r, cos_c, sin_c):
    n_tok = cos_r.shape[0]
    pad = jnp.zeros((n_tok, 128 - QK_ROPE), F32)
    rope_t = jnp.concatenate([cos_r, sin_r, cos_c, sin_c], axis=1).T
    cosk = jnp.concatenate([cos_r, cos_r, cos_c, cos_c, pad], axis=1)
    sink = jnp.concatenate([sin_r, sin_r, sin_c, sin_c, pad], axis=1)
    return rope_t, cosk, sink


def _mla_weights(w_in, g_qa, w_qup, g_kva, w_kvup, w_out):
    d = w_in.shape[0]
    kv_hi = Q_LORA + KV_LORA
    pe_hi = kv_hi + QK_ROPE
    w_pe = w_in[:, kv_hi:pe_hi]
    r = ROPE_QUARTER
    w_rot = jnp.concatenate([-w_pe[:, r:2 * r], w_pe[:, 0:r], -w_pe[:, 3 * r:4 * r], w_pe[:, 2 * r:3 * r]], axis=1)
    zpad = jnp.zeros((d, 128 - QK_ROPE), w_in.dtype)
    wa = jnp.concatenate([w_in[:, :kv_hi], w_pe, zpad, w_rot, zpad], axis=1).astype(BF16)
    wgt = w_in[:, pe_hi:].T.astype(BF16)
    wq = w_qup.reshape(Q_LORA, HEADS, QK_NOPE + QK_ROPE)
    wq = jnp.pad(wq, ((0, 0), (0, 0), (0, QK_PAD - QK_NOPE - QK_ROPE)))
    wqt = wq.reshape(Q_LORA, HEADS * QK_PAD).T.astype(BF16)
    wkv = w_kvup.reshape(KV_LORA, HEADS, QK_NOPE + V_DIM)
    wk = wkv[:, :, :QK_NOPE].reshape(KV_LORA, HEADS * QK_NOPE).astype(BF16)
    wvt = wkv[:, :, QK_NOPE:].reshape(KV_LORA, HEADS * V_DIM).T.astype(BF16)
    return (wa, wgt, g_qa.reshape(1, Q_LORA), wqt, g_kva.reshape(1, KV_LORA), wk, wvt), w_out.astype(BF16)


def kernel(x, c, ctx, c_ctx, norm_g, w_ada, b_ada, mla_w_in, mla_g_qa, mla_w_qup, mla_g_kva,
           mla_w_kvup, mla_w_out, fno_w_in, fno_w_out, final_g):
    bsz, n_tok, d = x.shape
    n_ctx = ctx.shape[1]
    depth = norm_g.shape[0]

    rows = 16
    cond =jnp.concatenate([c, c_ctx[None, :], jnp.zeros((rows - bsz - 1, d), F32)], axis=0)
    mods = _ada(cond, w_ada, b_ada)

    rope_lat = _pack_rope(*_rope_tables(n_tok))
    ones, zero = jnp.ones((n_ctx, ROPE_QUARTER), F32), jnp.zeros((n_ctx, ROPE_QUARTER), F32)
    rope_ctx = _pack_rope(ones, zero, ones, zero)
    dft_lat = _dft_tables(n_tok)
    dft_ctx = _dft_tables(n_ctx)
    fg = final_g.reshape(1, d)

    h_lat, h_ctx = x, ctx
    for i in range(depth):
        mixer, idx = i % 2, i // 2
        ctx_later = any(j % 2 == 0 for j in range(i + 1, depth))
        last = i == depth - 1
        g = norm_g[i].reshape(1, d)
        sh, sc, ga = (mods[i, :, j * d:(j + 1) * d] for j in range(3))
        lat = lambda a: a[:bsz].reshape(bsz, 1, d)
        cx = lambda a: jnp.broadcast_to(a[bsz].reshape(1, 1, d), (bsz, 1, d))
        if mixer == 0:
            wts, w_out = _mla_weights(mla_w_in[idx], mla_g_qa[idx], mla_w_qup[idx], mla_g_kva[idx],
                                      mla_w_kvup[idx], mla_w_out[idx])
            qt_l, k_l, vt_l, gt_l = _mla_pre(h_lat, g, lat(sc), lat(sh), wts, rope_lat, tm=512)
            qt_c, k_c, vt_c, gt_c = _mla_pre(h_ctx, g, cx(sc), cx(sh), wts, rope_ctx, tm=n_ctx)
            o_l = _attention_lat(qt_l, k_c, vt_c, k_l, vt_l, qb=512, kb=768)
            h_new = _mla_post(o_l.reshape(bsz, d, n_tok), gt_l, w_out, lat(ga), h_lat, tm=512)
            if ctx_later:
                o_c = _attention(qt_c, [(k_c, vt_c)], qb=n_ctx, kb=n_ctx)
                h_ctx = _mla_post(o_c.reshape(bsz, d, n_ctx), gt_c, w_out, cx(ga), h_ctx, tm=n_ctx)
            h_lat = h_new
        else:
            w_in = fno_w_in[idx].astype(BF16)
            w_out = fno_w_out[idx].astype(BF16)
            wre, wim, sg = _fno_pre(h_lat, g, lat(sc), lat(sh), w_in, dft_lat["cs"], tm=512)
            zre, zim = _dft1(wre, wim, dft_lat["f1"], dft_lat["twc"], dft_lat["tws"], g1=16, cols=512)
            y = _dft2(zre, zim, dft_lat["f2"], g2=16, cols=512)
            h_new = _fno_post(y, sg, w_out, lat(ga), h_lat, fg, tm=1024, final=last)
            if ctx_later:
                wre, wim, sg = _fno_pre(h_ctx, g, cx(sc), cx(sh), w_in, dft_ctx["cs"], tm=n_ctx)
                h_ctx = _dft2_post(wre, wim, dft_ctx["f2"], sg, w_out, cx(ga), h_ctx, fg,
                                   n1=dft_ctx["n1"], kk=1, final=False)
            h_lat = h_new
    return h_lat
```

```python
import functools
import math

import numpy as np
import jax
import jax.numpy as jnp
from jax import lax
from jax.experimental import pallas as pl
from jax.experimental.pallas import tpu as pltpu

F32 = jnp.float32
BF16 = jnp.bfloat16

D_MODEL = 1024
DEPTH = 4
GRID_W = 64
NORM_EPS = 1e-6
HEADS = 8
QK_NOPE = 128
QK_ROPE = 64
V_DIM = 128
Q_LORA = 256
KV_LORA = 128
ROPE_THETA = 10000.0
ROPE_QUARTER = QK_ROPE // 4
QK_PAD = 256
FOURIER_GROUPS = 8
GROUP_DIM = D_MODEL // FOURIER_GROUPS
DFT_INNER = 128
VMEM_LIMIT = 56 * 1024 * 1024

NT_DIMS = (((1,), (1,)), ((), ()))
TN_DIMS = (((0,), (0,)), ((), ()))


def _params(*semantics):
    return pltpu.CompilerParams(dimension_semantics=semantics, vmem_limit_bytes=VMEM_LIMIT)


def _silu(x):
    return x * jax.nn.sigmoid(x)


def _rms(x):
    return x * lax.rsqrt(jnp.mean(x * x, axis=-1, keepdims=True) + NORM_EPS)


def _modulated_norm(h_ref, g_ref, sc_ref, sh_ref):
    u = (_rms(h_ref[0]) * g_ref[...]) * (1.0 + sc_ref[0]) + sh_ref[0]
    return u.astype(BF16)


def _split_bf16(x):
    hi = x.astype(BF16)
    return hi, (x - hi.astype(F32)).astype(BF16)


def _ada_kernel(c_ref, w_ref, b_ref, o_ref):
    a_hi, a_lo = _split_bf16(_silu(c_ref[...]))
    w_hi, w_lo = _split_bf16(w_ref[0])
    dot = functools.partial(jnp.dot, preferred_element_type=F32)
    o_ref[0] = dot(a_hi, w_hi) + (dot(a_hi, w_lo) + dot(a_lo, w_hi)) + b_ref[0]


def _ada(cond, w_ada, b_ada):
    depth, d, d3 = w_ada.shape
    rows = cond.shape[0]
    return pl.pallas_call(
        _ada_kernel,
        grid=(depth, d3 // d),
        in_specs=[
            pl.BlockSpec((rows, d), lambda i, j: (0, 0)),
            pl.BlockSpec((1, d, d), lambda i, j: (i, 0, j)),
            pl.BlockSpec((1, 1, d), lambda i, j: (i, 0, j)),
        ],
        out_specs=pl.BlockSpec((1, rows, d), lambda i, j: (i, 0, j)),
        out_shape=jax.ShapeDtypeStruct((depth, rows, d3), F32),
        compiler_params=_params("arbitrary", "arbitrary"),
        name="ada",
    )(cond, w_ada, b_ada.reshape(depth, 1, d3))


def _mla_pre_kernel(h_ref, g_ref, sc_ref, sh_ref, wa_ref, wgt_ref, gqa_ref, wqt_ref, gkva_ref,
                    wk_ref, wvt_ref, ropet_ref, cosk_ref, sink_ref,
                    qt_ref, k_ref, vt_ref, gt_ref, *, q_scale):
    ub = _modulated_norm(h_ref, g_ref, sc_ref, sh_ref)
    pa = jnp.dot(ub, wa_ref[...], preferred_element_type=F32)
    c_q = pa[:, :Q_LORA]
    c_kv = pa[:, Q_LORA:Q_LORA + KV_LORA]
    kpe = pa[:, 384:512]
    kpe_rot = pa[:, 512:640]

    gate_t = lax.dot_general(wgt_ref[...], ub, NT_DIMS, preferred_element_type=F32)
    gt_ref[0] = _silu(gate_t).astype(BF16)

    cqn = (_rms(c_q) * gqa_ref[...]).astype(BF16)
    q_t = lax.dot_general(wqt_ref[...], cqn, NT_DIMS, preferred_element_type=F32)
    rope_t = ropet_ref[...]
    r = ROPE_QUARTER
    cos_r, sin_r, cos_c, sin_c = (rope_t[i * r:(i + 1) * r] for i in range(4))
    zeros = jnp.zeros((QK_PAD - QK_NOPE - QK_ROPE, q_t.shape[1]), BF16)
    for hd in range(HEADS):
        b0 = hd * QK_PAD
        qt_ref[0, hd, 0:QK_NOPE, :] = (q_t[b0:b0 + QK_NOPE] * q_scale).astype(BF16)
        p0 = b0 + QK_NOPE
        a, b, c, d = (q_t[p0 + i * r:p0 + (i + 1) * r] for i in range(4))
        rot = jnp.concatenate([a * cos_r - b * sin_r, b * cos_r + a * sin_r,
                               c * cos_c - d * sin_c, d * cos_c + c * sin_c], axis=0)
        qt_ref[0, hd, QK_NOPE:QK_NOPE + QK_ROPE, :] = (rot * q_scale).astype(BF16)
        qt_ref[0, hd, QK_NOPE + QK_ROPE:QK_PAD, :] = zeros

    ckvn = (_rms(c_kv) * gkva_ref[...]).astype(BF16)
    k_nope = jnp.dot(ckvn, wk_ref[...], preferred_element_type=F32)
    v_t = lax.dot_general(wvt_ref[...], ckvn, NT_DIMS, preferred_element_type=F32)
    k_rope = (kpe * cosk_ref[...] + kpe_rot * sink_ref[...]).astype(BF16)
    for hd in range(HEADS):
        k_ref[0, hd, :, 0:QK_NOPE] = k_nope[:, hd * QK_NOPE:(hd + 1) * QK_NOPE].astype(BF16)
        k_ref[0, hd, :, QK_NOPE:QK_PAD] = k_rope
        vt_ref[0, hd] = v_t[hd * V_DIM:(hd + 1) * V_DIM].astype(BF16)


def _mla_pre(h, g, sc, sh, wts, tables, tm):
    bsz, t, d = h.shape
    wa, wgt, gqa, wqt, gkva, wk, wvt = wts
    rope_t, cosk, sink = tables
    q_scale = math.log2(math.e) / math.sqrt(QK_NOPE + QK_ROPE)
    const = lambda shape: pl.BlockSpec(shape, lambda b, i: (0,) * len(shape))
    vec = pl.BlockSpec((1, 1, d), lambda b, i: (b, 0, 0))
    return pl.pallas_call(
        functools.partial(_mla_pre_kernel, q_scale=q_scale),
        grid=(bsz, t // tm),
        in_specs=[
            pl.BlockSpec((1, tm, d), lambda b, i: (b, i, 0)),
            const((1, d)), vec, vec,
            const(wa.shape), const(wgt.shape), const(gqa.shape), const(wqt.shape),
            const(gkva.shape), const(wk.shape), const(wvt.shape),
            pl.BlockSpec((QK_ROPE, tm), lambda b, i: (0, i)),
            pl.BlockSpec((tm, 128), lambda b, i: (i, 0)),
            pl.BlockSpec((tm, 128), lambda b, i: (i, 0)),
        ],
        out_specs=[
            pl.BlockSpec((1, HEADS, QK_PAD, tm), lambda b, i: (b, 0, 0, i)),
            pl.BlockSpec((1, HEADS, tm, QK_PAD), lambda b, i: (b, 0, i, 0)),
            pl.BlockSpec((1, HEADS, V_DIM, tm), lambda b, i: (b, 0, 0, i)),
            pl.BlockSpec((1, HEADS * V_DIM, tm), lambda b, i: (b, 0, i)),
        ],
        out_shape=[
            jax.ShapeDtypeStruct((bsz, HEADS, QK_PAD, t), BF16),
            jax.ShapeDtypeStruct((bsz, HEADS, t, QK_PAD), BF16),
            jax.ShapeDtypeStruct((bsz, HEADS, V_DIM, t), BF16),
            jax.ShapeDtypeStruct((bsz, HEADS * V_DIM, t), BF16),
        ],
        compiler_params=_params("arbitrary", "arbitrary"),
        name="mla_pre",
    )(h, g, sc, sh, wa, wgt, gqa, wqt, gkva, wk, wvt, rope_t, cosk, sink)


def _attn_chunk(kc, vc, q_t, m_sc, l_sc, acc_sc, first):
    s = jnp.dot(kc, q_t, preferred_element_type=F32)
    mc = jnp.max(s, axis=0, keepdims=True)
    if first:
        m_new = mc
        p = jnp.exp2(s - m_new)
        l_sc[...] = jnp.sum(p, axis=0, keepdims=True)
        acc_sc[...] = jnp.dot(vc, p.astype(BF16), preferred_element_type=F32)
    else:
        m_old = m_sc[...]
        m_new = jnp.maximum(m_old, mc)
        alpha = jnp.exp2(m_old - m_new)
        p = jnp.exp2(s - m_new)
        l_sc[...] = alpha * l_sc[...] + jnp.sum(p, axis=0, keepdims=True)
        acc_sc[...] = alpha * acc_sc[...] + jnp.dot(vc, p.astype(BF16), preferred_element_type=F32)
    m_sc[...] = m_new


def _attn_kernel(*refs, n_src, kb):
    qt_ref = refs[0]
    srcs = refs[1:1 + 2 * n_src]
    o_ref = refs[1 + 2 * n_src]
    m_sc, l_sc, acc_sc = refs[2 + 2 * n_src:]
    q_t = qt_ref[0, 0]
    for s in range(n_src):
        k_ref, vt_ref = srcs[2 * s], srcs[2 * s + 1]
        tk = k_ref.shape[2]
        ck = min(kb, tk)
        n_chunks = tk // ck
        start = 0
        if s == 0:
            _attn_chunk(k_ref[0, 0, 0:ck, :], vt_ref[0, 0, :, 0:ck], q_t, m_sc, l_sc, acc_sc, True)
            start = 1
        if n_chunks > start:
            def body(i, carry, k_ref=k_ref, vt_ref=vt_ref, ck=ck):
                off = pl.multiple_of(i * ck, ck)
                _attn_chunk(k_ref[0, 0, pl.ds(off, ck), :], vt_ref[0, 0, :, pl.ds(off, ck)],
                            q_t, m_sc, l_sc, acc_sc, False)
                return carry
            lax.fori_loop(start, n_chunks, body, 0)
    o_ref[0, 0] = (acc_sc[...] / l_sc[...]).astype(o_ref.dtype)


def _attention(q_t, sources, qb, kb):
    bsz, heads, _, t = q_t.shape
    in_specs = [pl.BlockSpec((1, 1, QK_PAD, qb), lambda b, h, i: (b, h, 0, i))]
    args = [q_t]
    for k, v_t in sources:
        tk = k.shape[2]
        in_specs.append(pl.BlockSpec((1, 1, tk, QK_PAD), lambda b, h, i: (b, h, 0, 0)))
        in_specs.append(pl.BlockSpec((1, 1, V_DIM, tk), lambda b, h, i: (b, h, 0, 0)))
        args += [k, v_t]
    return pl.pallas_call(
        functools.partial(_attn_kernel, n_src=len(sources), kb=kb),
        grid=(bsz, heads, t // qb),
        in_specs=in_specs,
        out_specs=pl.BlockSpec((1, 1, V_DIM, qb), lambda b, h, i: (b, h, 0, i)),
        out_shape=jax.ShapeDtypeStruct((bsz, heads, V_DIM, t), BF16),
        scratch_shapes=[pltpu.VMEM((1, qb), F32), pltpu.VMEM((1, qb), F32), pltpu.VMEM((V_DIM, qb), F32)],
        compiler_params=_params("arbitrary", "arbitrary", "arbitrary"),
        name="attention",
    )(*args)


def _softmax_pv(s_ref, pv_fn, m_sc, l_sc, acc_sc, first):
    s = s_ref[...]
    mc = jnp.max(s, axis=0, keepdims=True)
    if first:
        m_new = mc
        p = jnp.exp2(s - m_new)
        l_sc[...] = jnp.sum(p, axis=0, keepdims=True)
        acc_sc[...] = pv_fn(p.astype(BF16))
    else:
        m_old = m_sc[...]
        m_new = jnp.maximum(m_old, mc)
        alpha = jnp.exp2(m_old - m_new)
        p = jnp.exp2(s - m_new)
        l_sc[...] = alpha * l_sc[...] + jnp.sum(p, axis=0, keepdims=True)
        acc_sc[...] = alpha * acc_sc[...] + pv_fn(p.astype(BF16))
    m_sc[...] = m_new


def _attn_lat_kernel(qt_ref, kc_ref, vtc_ref, kl_ref, vtl_ref, o_ref, s_0, s_odd, s_even, m_sc, l_sc, acc_sc,
                     *, kb, qb, split):
    n_ctx = kc_ref.shape[2]
    n_lat = kl_ref.shape[2]
    n_chunks = (n_ctx + n_lat) // kb
    n_q = qt_ref.shape[3] // qb
    head = kb - n_ctx
    dot = functools.partial(jnp.dot, preferred_element_type=F32)

    def lat_offset(c):
        return c * kb - n_ctx

    def q_cols(qi):
        return pl.ds(pl.multiple_of(qi * qb, qb), qb)

    def buf(c):
        return s_0 if c == 0 else (s_odd if c % 2 == 1 else s_even)

    def scores(c, qi):
        q_t = qt_ref[0, 0, :, q_cols(qi)]
        if c == 0:
            s_0[0:n_ctx, :] = dot(kc_ref[0, 0], q_t)
            s_0[n_ctx:kb, :] = dot(kl_ref[0, 0, 0:head, :], q_t)
        else:
            buf(c)[...] = dot(kl_ref[0, 0, lat_offset(c):lat_offset(c) + kb, :], q_t)

    def consume(c):
        if c == 0:
            pv = lambda p: dot(vtc_ref[0, 0], p[0:n_ctx]) + dot(vtl_ref[0, 0, :, 0:head], p[n_ctx:kb])
        else:
            pv = lambda p: dot(vtl_ref[0, 0, :, lat_offset(c):lat_offset(c) + kb], p)
        _softmax_pv(buf(c), pv, m_sc, l_sc, acc_sc, c == 0)

    def steps(qi, lo, hi):
        for c in range(lo, hi):
            if c + 1 < n_chunks:
                scores(c + 1, qi)
            else:
                scores(0, jnp.minimum(qi + 1, n_q - 1))
            consume(c)

    one_trip = jnp.minimum(pl.program_id(0) + 1, 1)
    scores(0, 0)

    def q_block(qi, carry):
        def first_half(_, carry):
            steps(qi, 0, split)
            return carry
        lax.fori_loop(0, one_trip, first_half, 0)
        steps(qi, split, n_chunks)
        o_ref[0, 0, :, q_cols(qi)] = (acc_sc[...] / l_sc[...]).astype(o_ref.dtype)
        return carry
    lax.fori_loop(0, n_q, q_block, 0)


def _attention_lat(q_t, k_c, vt_c, k_l, vt_l, qb, kb):
    bsz, heads, _, t = q_t.shape
    n_ctx, n_lat = k_c.shape[2], k_l.shape[2]
    n_chunks = (n_ctx + n_lat) // kb
    assert n_chunks * kb == n_ctx + n_lat and n_chunks >= 2 and n_ctx < kb and t % qb == 0
    per_head = lambda shape: pl.BlockSpec((1, 1) + shape, lambda b, h: (b, h, 0, 0))
    return pl.pallas_call(
        functools.partial(_attn_lat_kernel, kb=kb, qb=qb, split=n_chunks // 2),
        grid=(bsz, heads),
        in_specs=[per_head((QK_PAD, t)), per_head((n_ctx, QK_PAD)), per_head((V_DIM, n_ctx)),
                  per_head((n_lat, QK_PAD)), per_head((V_DIM, n_lat))],
        out_specs=per_head((V_DIM, t)),
        out_shape=jax.ShapeDtypeStruct((bsz, heads, V_DIM, t), BF16),
        scratch_shapes=[pltpu.VMEM((kb, qb), F32), pltpu.VMEM((kb, qb), F32), pltpu.VMEM((kb, qb), F32),
                        pltpu.VMEM((1, qb), F32), pltpu.VMEM((1, qb), F32), pltpu.VMEM((V_DIM, qb), F32)],
        compiler_params=_params("arbitrary", "arbitrary"),
        name="attention_lat",
    )(q_t, k_c, vt_c, k_l, vt_l)


def _mla_post_kernel(ot_ref, gt_ref, w_ref, ga_ref, h_ref, out_ref):
    og = (ot_ref[0].astype(F32) * gt_ref[0].astype(F32)).astype(BF16)
    y = lax.dot_general(og, w_ref[...], TN_DIMS, preferred_element_type=F32)
    out_ref[0] = h_ref[0] + ga_ref[0] * y


def _mla_post(o_t, g_t, w_out, ga, h, tm):
    bsz, t, d = h.shape
    return pl.pallas_call(
        _mla_post_kernel,
        grid=(bsz, t // tm),
        in_specs=[
            pl.BlockSpec((1, d, tm), lambda b, i: (b, 0, i)),
            pl.BlockSpec((1, d, tm), lambda b, i: (b, 0, i)),
            pl.BlockSpec((d, d), lambda b, i: (0, 0)),
            pl.BlockSpec((1, 1, d), lambda b, i: (b, 0, 0)),
            pl.BlockSpec((1, tm, d), lambda b, i: (b, i, 0)),
        ],
        out_specs=pl.BlockSpec((1, tm, d), lambda b, i: (b, i, 0)),
        out_shape=jax.ShapeDtypeStruct((bsz, t, d), F32),
        compiler_params=_params("arbitrary", "arbitrary"),
        name="mla_post",
    )(o_t, g_t, w_out, ga, h)


def _fno_pre_kernel(h_ref, g_ref, sc_ref, sh_ref, w_ref, cs_ref, wre_ref, wim_ref, sg_ref):
    ub = _modulated_norm(h_ref, g_ref, sc_ref, sh_ref)
    proj = jnp.dot(ub, w_ref[...], preferred_element_type=F32)
    d = sg_ref.shape[2]
    sg_ref[0] = _silu(proj[:, d:]).astype(BF16)
    zb = proj[:, :d].astype(BF16)
    for g in range(FOURIER_GROUPS):
        lo, hi = g * GROUP_DIM, (g + 1) * GROUP_DIM
        ab = jnp.dot(zb[:, lo:hi], cs_ref[...], preferred_element_type=F32)
        wre_ref[0, :, lo:hi] = ab[:, :GROUP_DIM].astype(BF16)
        wim_ref[0, :, lo:hi] = ab[:, GROUP_DIM:].astype(BF16)


def _fno_pre(h, g, sc, sh, w_in, cs, tm):
    bsz, t, d = h.shape
    vec = pl.BlockSpec((1, 1, d), lambda b, i: (b, 0, 0))
    tok = pl.BlockSpec((1, tm, d), lambda b, i: (b, i, 0))
    out = jax.ShapeDtypeStruct((bsz, t, d), BF16)
    return pl.pallas_call(
        _fno_pre_kernel,
        grid=(bsz, t // tm),
        in_specs=[tok, pl.BlockSpec((1, d), lambda b, i: (0, 0)), vec, vec,
                  pl.BlockSpec(w_in.shape, lambda b, i: (0, 0)),
                  pl.BlockSpec(cs.shape, lambda b, i: (0, 0))],
        out_specs=[tok, tok, tok],
        out_shape=[out, out, out],
        compiler_params=_params("arbitrary", "arbitrary"),
        name="fno_pre",
    )(h, g, sc, sh, w_in, cs)


def _dft1_kernel(xre_ref, xim_ref, f_ref, twc_ref, tws_ref, zre_ref, zim_ref):
    xre = jnp.swapaxes(xre_ref[0], 0, 1)
    xim = jnp.swapaxes(xim_ref[0], 0, 1)
    n2 = xre.shape[1]
    for j in range(xre.shape[0]):
        x = jnp.concatenate([xre[j], xim[j]], axis=0)
        z = jnp.dot(f_ref[...], x, preferred_element_type=F32)
        zr, zi = z[:n2], z[n2:]
        c, s = twc_ref[j], tws_ref[j]
        zre_ref[0, j] = (zr * c - zi * s).astype(BF16)
        zim_ref[0, j] = (zr * s + zi * c).astype(BF16)


def _dft1(wre, wim, f1, twc, tws, g1, cols):
    bsz, t, d = wre.shape
    n2 = t // DFT_INNER
    assert n2 * DFT_INNER == t and DFT_INNER % g1 == 0 and d % cols == 0
    xre = wre.reshape(bsz, n2, DFT_INNER, d)
    xim = wim.reshape(bsz, n2, DFT_INNER, d)
    blk_in = pl.BlockSpec((1, n2, g1, cols), lambda b, i, j: (b, 0, i, j))
    blk_out = pl.BlockSpec((1, g1, n2, cols), lambda b, i, j: (b, i, 0, j))
    tw = pl.BlockSpec((g1, n2, 1), lambda b, i, j: (i, 0, 0))
    out = jax.ShapeDtypeStruct((bsz, DFT_INNER, n2, d), BF16)
    return pl.pallas_call(
        _dft1_kernel,
        grid=(bsz, DFT_INNER // g1, d // cols),
        in_specs=[blk_in, blk_in, pl.BlockSpec(f1.shape, lambda b, i, j: (0, 0)), tw, tw],
        out_specs=[blk_out, blk_out],
        out_shape=[out, out],
        compiler_params=_params("arbitrary", "arbitrary", "arbitrary"),
        name="dft1",
    )(xre, xim, f1, twc, tws)


def _dft2_kernel(zre_ref, zim_ref, f_ref, y_ref, *, scale):
    zre = jnp.swapaxes(zre_ref[0], 0, 1)
    zim = jnp.swapaxes(zim_ref[0], 0, 1)
    ys = []
    for j in range(zre.shape[0]):
        zc = jnp.concatenate([zre[j], zim[j]], axis=0)
        ys.append(jnp.dot(f_ref[...], zc, preferred_element_type=F32) * scale)
    y_ref[0] = jnp.swapaxes(jnp.stack(ys, axis=0), 0, 1).astype(y_ref.dtype)


def _dft2(zre, zim, f2, g2, cols):
    bsz, n1, n2, d = zre.shape
    assert n2 % g2 == 0 and d % cols == 0
    scale = 1.0 / math.sqrt(n1 * n2 * GROUP_DIM)
    blk = pl.BlockSpec((1, n1, g2, cols), lambda b, i, j: (b, 0, i, j))
    y = pl.pallas_call(
        functools.partial(_dft2_kernel, scale=scale),
        grid=(bsz, n2 // g2, d // cols),
        in_specs=[blk, blk, pl.BlockSpec(f2.shape, lambda b, i, j: (0, 0))],
        out_specs=blk,
        out_shape=jax.ShapeDtypeStruct((bsz, n1, n2, d), BF16),
        compiler_params=_params("arbitrary", "arbitrary", "arbitrary"),
        name="dft2",
    )(zre, zim, f2)
    return y.reshape(bsz, n1 * n2, d)


def _fno_post_kernel(y_ref, sg_ref, w_ref, ga_ref, h_ref, fg_ref, out_ref, *, final):
    yg = (y_ref[0].astype(F32) * sg_ref[0].astype(F32)).astype(BF16)
    hn = h_ref[0] + ga_ref[0] * jnp.dot(yg, w_ref[...], preferred_element_type=F32)
    if final:
        hn = _rms(hn) * fg_ref[...]
    out_ref[0] = hn


def _fno_post(y, sg, w_out, ga, h, final_g, tm, final):
    bsz, t, d = h.shape
    tok = pl.BlockSpec((1, tm, d), lambda b, i: (b, i, 0))
    return pl.pallas_call(
        functools.partial(_fno_post_kernel, final=final),
        grid=(bsz, t // tm),
        in_specs=[tok, tok, pl.BlockSpec((d, d), lambda b, i: (0, 0)),
                  pl.BlockSpec((1, 1, d), lambda b, i: (b, 0, 0)), tok,
                  pl.BlockSpec((1, d), lambda b, i: (0, 0))],
        out_specs=tok,
        out_shape=jax.ShapeDtypeStruct((bsz, t, d), F32),
        compiler_params=_params("arbitrary", "arbitrary"),
        name="fno_post",
    )(y, sg, w_out, ga, h, final_g)


def _dft2_post_kernel(zre_ref, zim_ref, f_ref, sg_ref, w_ref, ga_ref, h_ref, fg_ref, out_ref,
                      *, kk, scale, final):
    d = w_ref.shape[0]
    for j in range(kk):
        zc = jnp.concatenate([zre_ref[0, j], zim_ref[0, j]], axis=0)
        y = jnp.dot(f_ref[...], zc, preferred_element_type=F32) * scale
        sg = sg_ref[0, :, j * d:(j + 1) * d].astype(F32)
        o = jnp.dot((y * sg).astype(BF16), w_ref[...], preferred_element_type=F32)
        hn = h_ref[0, :, j * d:(j + 1) * d] + ga_ref[0] * o
        if final:
            hn = _rms(hn) * fg_ref[...]
        out_ref[0, :, j * d:(j + 1) * d] = hn


def _dft2_post(zre, zim, f2, sg, w_out, ga, h, final_g, n1, kk, final):
    bsz, t, d = h.shape
    n2 = t // n1
    zre = zre.reshape(bsz, n2, n1, d)
    zim = zim.reshape(bsz, n2, n1, d)
    zblk = pl.BlockSpec((1, kk, n1, d), lambda b, i: (b, i, 0, 0))
    tok = pl.BlockSpec((1, n1, kk * d), lambda b, i: (b, 0, i))
    scale = 1.0 / math.sqrt(t * GROUP_DIM)
    out = pl.pallas_call(
        functools.partial(_dft2_post_kernel, kk=kk, scale=scale, final=final),
        grid=(bsz, n2 // kk),
        in_specs=[zblk, zblk, pl.BlockSpec(f2.shape, lambda b, i: (0, 0)), tok,
                  pl.BlockSpec((d, d), lambda b, i: (0, 0)),
                  pl.BlockSpec((1, 1, d), lambda b, i: (b, 0, 0)), tok,
                  pl.BlockSpec((1, d), lambda b, i: (0, 0))],
        out_specs=tok,
        out_shape=jax.ShapeDtypeStruct((bsz, n1, n2 * d), F32),
        compiler_params=_params("arbitrary", "arbitrary"),
        name="dft2_post",
    )(zre, zim, f2, sg.reshape(bsz, n1, n2 * d), w_out, ga, h.reshape(bsz, n1, n2 * d), final_g)
    return out.reshape(bsz, t, d)


def _dft_cos_sin(n):
    idx = np.arange(n)
    ang = 2.0 * np.pi * ((idx[:, None] * idx[None, :]) % n) / n
    return np.cos(ang), np.sin(ang)


def _dft_tables(t):
    cn, sn = _dft_cos_sin(GROUP_DIM)
    cs = np.concatenate([cn, sn], axis=1)
    if t <= 256:
        n1, n2 = t, 1
    else:
        n1, n2 = DFT_INNER, t // DFT_INNER
    c1, s1 = _dft_cos_sin(n1)
    f2 = np.concatenate([c1, -s1], axis=1)
    c2, s2 = _dft_cos_sin(n2)
    f1 = np.block([[c2, -s2], [s2, c2]])
    k2 = np.arange(n2)[None, :, None]
    t1 = np.arange(n1)[:, None, None]
    ang = 2.0 * np.pi * ((k2 * t1) % t) / t
    as_f32 = lambda a: jnp.asarray(a.astype(np.float32))
    return dict(n1=n1, n2=n2, cs=as_f32(cs).astype(BF16), f1=as_f32(f1).astype(BF16),
                f2=as_f32(f2).astype(BF16), twc=as_f32(np.cos(ang)), tws=as_f32(np.sin(ang)))


def _rope_tables(n_tok):
    inv_freq = 1.0 / (ROPE_THETA ** (jnp.arange(0, 2 * ROPE_QUARTER, 2, dtype=F32) / (2 * ROPE_QUARTER)))
    pos = jnp.arange(n_tok, dtype=jnp.int32)
    ang_r = (pos // GRID_W).astype(F32)[:, None] * inv_freq[None, :]
    ang_c = (pos % GRID_W).astype(F32)[:, None] * inv_freq[None, :]
    return jnp.cos(ang_r), jnp.sin(ang_r), jnp.cos(ang_c), jnp.sin(ang_c)


def _pack_rope(cos_r, sin_r, cos_c, sin_c):
    n_tok = cos_r.shape[0]
    pad = jnp.zeros((n_tok, 128 - QK_ROPE), F32)
    rope_t = jnp.concatenate([cos_r, sin_r, cos_c, sin_c], axis=1).T
    cosk = jnp.concatenate([cos_r, cos_r, cos_c, cos_c, pad], axis=1)
    sink = jnp.concatenate([sin_r, sin_r, sin_c, sin_c, pad], axis=1)
    return rope_t, cosk, sink


def _mla_weights(w_in, g_qa, w_qup, g_kva, w_kvup, w_out):
    d = w_in.shape[0]
    kv_hi = Q_LORA + KV_LORA
    pe_hi = kv_hi + QK_ROPE
    w_pe = w_in[:, kv_hi:pe_hi]
    r = ROPE_QUARTER
    w_rot = jnp.concatenate([-w_pe[:, r:2 * r], w_pe[:, 0:r], -w_pe[:, 3 * r:4 * r], w_pe[:, 2 * r:3 * r]], axis=1)
    zpad = jnp.zeros((d, 128 - QK_ROPE), w_in.dtype)
    wa = jnp.concatenate([w_in[:, :kv_hi], w_pe, zpad, w_rot, zpad], axis=1).astype(BF16)
    wgt = w_in[:, pe_hi:].T.astype(BF16)
    wq = w_qup.reshape(Q_LORA, HEADS, QK_NOPE + QK_ROPE)
    wq = jnp.pad(wq, ((0, 0), (0, 0), (0, QK_PAD - QK_NOPE - QK_ROPE)))
    wqt = wq.reshape(Q_LORA, HEADS * QK_PAD).T.astype(BF16)
    wkv = w_kvup.reshape(KV_LORA, HEADS, QK_NOPE + V_DIM)
    wk = wkv[:, :, :QK_NOPE].reshape(KV_LORA, HEADS * QK_NOPE).astype(BF16)
    wvt = wkv[:, :, QK_NOPE:].reshape(KV_LORA, HEADS * V_DIM).T.astype(BF16)
    return (wa, wgt, g_qa.reshape(1, Q_LORA), wqt, g_kva.reshape(1, KV_LORA), wk, wvt), w_out.astype(BF16)


def kernel(x, c, ctx, c_ctx, norm_g, w_ada, b_ada, mla_w_in, mla_g_qa, mla_w_qup, mla_g_kva,
           mla_w_kvup, mla_w_out, fno_w_in, fno_w_out, final_g):
    bsz, n_tok, d = x.shape
    n_ctx = ctx.shape[1]
    depth = norm_g.shape[0]

    rows = 16
    cond =jnp.concatenate([c, c_ctx[None, :], jnp.zeros((rows - bsz - 1, d), F32)], axis=0)
    mods = _ada(cond, w_ada, b_ada)

    rope_lat = _pack_rope(*_rope_tables(n_tok))
    ones, zero = jnp.ones((n_ctx, ROPE_QUARTER), F32), jnp.zeros((n_ctx, ROPE_QUARTER), F32)
    rope_ctx = _pack_rope(ones, zero, ones, zero)
    dft_lat = _dft_tables(n_tok)
    dft_ctx = _dft_tables(n_ctx)
    fg = final_g.reshape(1, d)

    h_lat, h_ctx = x, ctx
    for i in range(depth):
        mixer, idx = i % 2, i // 2
        ctx_later = any(j % 2 == 0 for j in range(i + 1, depth))
        last = i == depth - 1
        g = norm_g[i].reshape(1, d)
        sh, sc, ga = (mods[i, :, j * d:(j + 1) * d] for j in range(3))
        lat = lambda a: a[:bsz].reshape(bsz, 1, d)
        cx = lambda a: jnp.broadcast_to(a[bsz].reshape(1, 1, d), (bsz, 1, d))
        if mixer == 0:
            wts, w_out = _mla_weights(mla_w_in[idx], mla_g_qa[idx], mla_w_qup[idx], mla_g_kva[idx],
                                      mla_w_kvup[idx], mla_w_out[idx])
            qt_l, k_l, vt_l, gt_l = _mla_pre(h_lat, g, lat(sc), lat(sh), wts, rope_lat, tm=512)
            qt_c, k_c, vt_c, gt_c = _mla_pre(h_ctx, g, cx(sc), cx(sh), wts, rope_ctx, tm=n_ctx)
            o_l = _attention_lat(qt_l, k_c, vt_c, k_l, vt_l, qb=512, kb=768)
            h_new = _mla_post(o_l.reshape(bsz, d, n_tok), gt_l, w_out, lat(ga), h_lat, tm=512)
            if ctx_later:
                o_c = _attention(qt_c, [(k_c, vt_c)], qb=n_ctx, kb=n_ctx)
                h_ctx = _mla_post(o_c.reshape(bsz, d, n_ctx), gt_c, w_out, cx(ga), h_ctx, tm=n_ctx)
            h_lat = h_new
        else:
            w_in = fno_w_in[idx].astype(BF16)
            w_out = fno_w_out[idx].astype(BF16)
            wre, wim, sg = _fno_pre(h_lat, g, lat(sc), lat(sh), w_in, dft_lat["cs"], tm=512)
            zre, zim = _dft1(wre, wim, dft_lat["f1"], dft_lat["twc"], dft_lat["tws"], g1=16, cols=512)
            y = _dft2(zre, zim, dft_lat["f2"], g2=16, cols=512)
            h_new = _fno_post(y, sg, w_out, lat(ga), h_lat, fg, tm=1024, final=last)
            if ctx_later:
                wre, wim, sg = _fno_pre(h_ctx, g, cx(sc), cx(sh), w_in, dft_ctx["cs"], tm=n_ctx)
                h_ctx = _dft2_post(wre, wim, dft_ctx["f2"], sg, w_out, cx(ga), h_ctx, fg,
                                   n1=dft_ctx["n1"], kk=1, final=False)
            h_lat = h_new
    return h_lat
```

```python
import functools
import math

import numpy as np
import jax
import jax.numpy as jnp
from jax import lax
from jax.experimental import pallas as pl
from jax.experimental.pallas import tpu as pltpu

F32 = jnp.float32
BF16 = jnp.bfloat16

D_MODEL = 1024
DEPTH = 4
GRID_W = 64
NORM_EPS = 1e-6
HEADS = 8
QK_NOPE = 128
QK_ROPE = 64
V_DIM = 128
Q_LORA = 256
KV_LORA = 128
ROPE_THETA = 10000.0
ROPE_QUARTER = QK_ROPE // 4
QK_PAD = 256
FOURIER_GROUPS = 8
GROUP_DIM = D_MODEL // FOURIER_GROUPS
DFT_INNER = 128
VMEM_LIMIT = 56 * 1024 * 1024

NT_DIMS = (((1,), (1,)), ((), ()))
TN_DIMS = (((0,), (0,)), ((), ()))


def _params(*semantics):
    return pltpu.CompilerParams(dimension_semantics=semantics, vmem_limit_bytes=VMEM_LIMIT)


def _silu(x):
    return x * jax.nn.sigmoid(x)


def _rms(x):
    return x * lax.rsqrt(jnp.mean(x * x, axis=-1, keepdims=True) + NORM_EPS)


def _modulated_norm(x, g_ref, sc_ref, sh_ref):
    u = (_rms(x) * g_ref[...]) * (1.0 + sc_ref[0]) + sh_ref[0]
    return u.astype(BF16)


def _split_bf16(x):
    hi = x.astype(BF16)
    return hi, (x - hi.astype(F32)).astype(BF16)


def _ada_kernel(c_ref, w_ref, b_ref, o_ref):
    a_hi, a_lo = _split_bf16(_silu(c_ref[...]))
    w_hi, w_lo = _split_bf16(w_ref[0])
    dot = functools.partial(jnp.dot, preferred_element_type=F32)
    o_ref[0] = dot(a_hi, w_hi) + (dot(a_hi, w_lo) + dot(a_lo, w_hi)) + b_ref[0]


def _ada(cond, w_ada, b_ada):
    depth, d, d3 = w_ada.shape
    rows = cond.shape[0]
    return pl.pallas_call(
        _ada_kernel,
        grid=(depth, d3 // d),
        in_specs=[
            pl.BlockSpec((rows, d), lambda i, j: (0, 0)),
            pl.BlockSpec((1, d, d), lambda i, j: (i, 0, j)),
            pl.BlockSpec((1, 1, d), lambda i, j: (i, 0, j)),
        ],
        out_specs=pl.BlockSpec((1, rows, d), lambda i, j: (i, 0, j)),
        out_shape=jax.ShapeDtypeStruct((depth, rows, d3), F32),
        compiler_params=_params("arbitrary", "arbitrary"),
        name="ada",
    )(cond, w_ada, b_ada.reshape(depth, 1, d3))


def _mla_pre_body(x, g_ref, sc_ref, sh_ref, wa_ref, wgt_ref, gqa_ref, wqt_ref, gkva_ref,
                  wk_ref, wvt_ref, ropet_ref, cosk_ref, sink_ref,
                  qt_ref, k_ref, vt_ref, gt_ref, *, q_scale):
    ub = _modulated_norm(x, g_ref, sc_ref, sh_ref)
    pa = jnp.dot(ub, wa_ref[...], preferred_element_type=F32)
    c_q = pa[:, :Q_LORA]
    c_kv = pa[:, Q_LORA:Q_LORA + KV_LORA]
    kpe = pa[:, 384:512]
    kpe_rot = pa[:, 512:640]

    gate_t = lax.dot_general(wgt_ref[...], ub, NT_DIMS, preferred_element_type=F32)
    gt_ref[0] = _silu(gate_t).astype(BF16)

    cqn = (_rms(c_q) * gqa_ref[...]).astype(BF16)
    q_t = lax.dot_general(wqt_ref[...], cqn, NT_DIMS, preferred_element_type=F32)
    rope_t = ropet_ref[...]
    r = ROPE_QUARTER
    cos_r, sin_r, cos_c, sin_c = (rope_t[i * r:(i + 1) * r] for i in range(4))
    zeros = jnp.zeros((QK_PAD - QK_NOPE - QK_ROPE, q_t.shape[1]), BF16)
    for hd in range(HEADS):
        b0 = hd * QK_PAD
        qt_ref[0, hd, 0:QK_NOPE, :] = (q_t[b0:b0 + QK_NOPE] * q_scale).astype(BF16)
        p0 = b0 + QK_NOPE
        a, b, c, d = (q_t[p0 + i * r:p0 + (i + 1) * r] for i in range(4))
        rot = jnp.concatenate([a * cos_r - b * sin_r, b * cos_r + a * sin_r,
                               c * cos_c - d * sin_c, d * cos_c + c * sin_c], axis=0)
        qt_ref[0, hd, QK_NOPE:QK_NOPE + QK_ROPE, :] = (rot * q_scale).astype(BF16)
        qt_ref[0, hd, QK_NOPE + QK_ROPE:QK_PAD, :] = zeros

    ckvn = (_rms(c_kv) * gkva_ref[...]).astype(BF16)
    k_nope = jnp.dot(ckvn, wk_ref[...], preferred_element_type=F32)
    v_t = lax.dot_general(wvt_ref[...], ckvn, NT_DIMS, preferred_element_type=F32)
    k_rope = (kpe * cosk_ref[...] + kpe_rot * sink_ref[...]).astype(BF16)
    for hd in range(HEADS):
        k_ref[0, hd, :, 0:QK_NOPE] = k_nope[:, hd * QK_NOPE:(hd + 1) * QK_NOPE].astype(BF16)
        k_ref[0, hd, :, QK_NOPE:QK_PAD] = k_rope
        vt_ref[0, hd] = v_t[hd * V_DIM:(hd + 1) * V_DIM].astype(BF16)


def _mla_pre_kernel(h_ref, *refs, q_scale):
    _mla_pre_body(h_ref[0], *refs, q_scale=q_scale)


N_MLA_PRE_INPUTS = 13


def _fno_post_mla_pre_kernel(y_ref, sg_ref, w_ref, ga_ref, h_ref, *refs, q_scale):
    pre_in, hn_ref, pre_out = refs[:N_MLA_PRE_INPUTS], refs[N_MLA_PRE_INPUTS], refs[N_MLA_PRE_INPUTS + 1:]
    hn = _fno_post_value(y_ref, sg_ref, w_ref, ga_ref, h_ref)
    hn_ref[0] = hn
    _mla_pre_body(hn, *pre_in, *pre_out, q_scale=q_scale)


def _mla_pre(h, g, sc, sh, wts, tables, tm, post=None):
    bsz, t, d = h.shape
    wa, wgt, gqa, wqt, gkva, wk, wvt = wts
    rope_t, cosk, sink = tables
    q_scale = math.log2(math.e) / math.sqrt(QK_NOPE + QK_ROPE)
    const = lambda shape: pl.BlockSpec(shape, lambda b, i: (0,) * len(shape))
    vec = pl.BlockSpec((1, 1, d), lambda b, i: (b, 0, 0))
    tok = pl.BlockSpec((1, tm, d), lambda b, i: (b, i, 0))
    in_specs = [
        tok, const((1, d)), vec, vec,
        const(wa.shape), const(wgt.shape), const(gqa.shape), const(wqt.shape),
        const(gkva.shape), const(wk.shape), const(wvt.shape),
        pl.BlockSpec((QK_ROPE, tm), lambda b, i: (0, i)),
        pl.BlockSpec((tm, 128), lambda b, i: (i, 0)),
        pl.BlockSpec((tm, 128), lambda b, i: (i, 0)),
    ]
    out_specs = [
        pl.BlockSpec((1, HEADS, QK_PAD, tm), lambda b, i: (b, 0, 0, i)),
        pl.BlockSpec((1, HEADS, tm, QK_PAD), lambda b, i: (b, 0, i, 0)),
        pl.BlockSpec((1, HEADS, V_DIM, tm), lambda b, i: (b, 0, 0, i)),
        pl.BlockSpec((1, HEADS * V_DIM, tm), lambda b, i: (b, 0, i)),
    ]
    out_shape = [
        jax.ShapeDtypeStruct((bsz, HEADS, QK_PAD, t), BF16),
        jax.ShapeDtypeStruct((bsz, HEADS, t, QK_PAD), BF16),
        jax.ShapeDtypeStruct((bsz, HEADS, V_DIM, t), BF16),
        jax.ShapeDtypeStruct((bsz, HEADS * V_DIM, t), BF16),
    ]
    args = (h, g, sc, sh, wa, wgt, gqa, wqt, gkva, wk, wvt, rope_t, cosk, sink)
    body, name = _mla_pre_kernel, "mla_pre"
    if post is not None:
        y, sg, w_out, ga = post
        in_specs = [tok, tok, const((d, d)), vec] + in_specs
        out_specs = [tok] + out_specs
        out_shape = [jax.ShapeDtypeStruct((bsz, t, d), F32)] + out_shape
        args = (y, sg, w_out, ga) + args
        body, name = _fno_post_mla_pre_kernel, "fno_post_mla_pre"
    return pl.pallas_call(
        functools.partial(body, q_scale=q_scale),
        grid=(bsz, t // tm),
        in_specs=in_specs,
        out_specs=out_specs,
        out_shape=out_shape,
        compiler_params=_params("arbitrary", "arbitrary"),
        name=name,
    )(*args)


def _attn_chunk(kc, vc, q_t, m_sc, l_sc, acc_sc, first):
    s = jnp.dot(kc, q_t, preferred_element_type=F32)
    mc = jnp.max(s, axis=0, keepdims=True)
    if first:
        m_new = mc
        p = jnp.exp2(s - m_new)
        l_sc[...] = jnp.sum(p, axis=0, keepdims=True)
        acc_sc[...] = jnp.dot(vc, p.astype(BF16), preferred_element_type=F32)
    else:
        m_old = m_sc[...]
        m_new = jnp.maximum(m_old, mc)
        alpha = jnp.exp2(m_old - m_new)
        p = jnp.exp2(s - m_new)
        l_sc[...] = alpha * l_sc[...] + jnp.sum(p, axis=0, keepdims=True)
        acc_sc[...] = alpha * acc_sc[...] + jnp.dot(vc, p.astype(BF16), preferred_element_type=F32)
    m_sc[...] = m_new


def _attn_kernel(*refs, n_src, kb):
    qt_ref = refs[0]
    srcs = refs[1:1 + 2 * n_src]
    o_ref = refs[1 + 2 * n_src]
    m_sc, l_sc, acc_sc = refs[2 + 2 * n_src:]
    q_t = qt_ref[0, 0]
    for s in range(n_src):
        k_ref, vt_ref = srcs[2 * s], srcs[2 * s + 1]
        tk = k_ref.shape[2]
        ck = min(kb, tk)
        n_chunks = tk // ck
        start = 0
        if s == 0:
            _attn_chunk(k_ref[0, 0, 0:ck, :], vt_ref[0, 0, :, 0:ck], q_t, m_sc, l_sc, acc_sc, True)
            start = 1
        if n_chunks > start:
            def body(i, carry, k_ref=k_ref, vt_ref=vt_ref, ck=ck):
                off = pl.multiple_of(i * ck, ck)
                _attn_chunk(k_ref[0, 0, pl.ds(off, ck), :], vt_ref[0, 0, :, pl.ds(off, ck)],
                            q_t, m_sc, l_sc, acc_sc, False)
                return carry
            lax.fori_loop(start, n_chunks, body, 0)
    o_ref[0, 0] = (acc_sc[...] / l_sc[...]).astype(o_ref.dtype)


def _attention(q_t, sources, qb, kb):
    bsz, heads, _, t = q_t.shape
    in_specs = [pl.BlockSpec((1, 1, QK_PAD, qb), lambda b, h, i: (b, h, 0, i))]
    args = [q_t]
    for k, v_t in sources:
        tk = k.shape[2]
        in_specs.append(pl.BlockSpec((1, 1, tk, QK_PAD), lambda b, h, i: (b, h, 0, 0)))
        in_specs.append(pl.BlockSpec((1, 1, V_DIM, tk), lambda b, h, i: (b, h, 0, 0)))
        args += [k, v_t]
    return pl.pallas_call(
        functools.partial(_attn_kernel, n_src=len(sources), kb=kb),
        grid=(bsz, heads, t // qb),
        in_specs=in_specs,
        out_specs=pl.BlockSpec((1, 1, V_DIM, qb), lambda b, h, i: (b, h, 0, i)),
        out_shape=jax.ShapeDtypeStruct((bsz, heads, V_DIM, t), BF16),
        scratch_shapes=[pltpu.VMEM((1, qb), F32), pltpu.VMEM((1, qb), F32), pltpu.VMEM((V_DIM, qb), F32)],
        compiler_params=_params("arbitrary", "arbitrary", "arbitrary"),
        name="attention",
    )(*args)


def _softmax_pv(s_ref, pv_fn, m_sc, l_sc, acc_sc, first):
    s = s_ref[...]
    mc = jnp.max(s, axis=0, keepdims=True)
    if first:
        m_new = mc
        p = jnp.exp2(s - m_new)
        l_sc[...] = jnp.sum(p, axis=0, keepdims=True)
        acc_sc[...] = pv_fn(p.astype(BF16))
    else:
        m_old = m_sc[...]
        m_new = jnp.maximum(m_old, mc)
        alpha = jnp.exp2(m_old - m_new)
        p = jnp.exp2(s - m_new)
        l_sc[...] = alpha * l_sc[...] + jnp.sum(p, axis=0, keepdims=True)
        acc_sc[...] = alpha * acc_sc[...] + pv_fn(p.astype(BF16))
    m_sc[...] = m_new


def _attn_lat_kernel(qt_ref, kc_ref, vtc_ref, kl_ref, vtl_ref, o_ref, s_0, s_odd, s_even, m_sc, l_sc, acc_sc,
                     *, kb, qb, split):
    n_ctx = kc_ref.shape[2]
    n_lat = kl_ref.shape[2]
    n_chunks = (n_ctx + n_lat) // kb
    n_q = qt_ref.shape[3] // qb
    head = kb - n_ctx
    dot = functools.partial(jnp.dot, preferred_element_type=F32)

    def lat_offset(c):
        return c * kb - n_ctx

    def q_cols(qi):
        return pl.ds(pl.multiple_of(qi * qb, qb), qb)

    def buf(c):
        return s_0 if c == 0 else (s_odd if c % 2 == 1 else s_even)

    def scores(c, qi):
        q_t = qt_ref[0, 0, :, q_cols(qi)]
        if c == 0:
            s_0[0:n_ctx, :] = dot(kc_ref[0, 0], q_t)
            s_0[n_ctx:kb, :] = dot(kl_ref[0, 0, 0:head, :], q_t)
        else:
            buf(c)[...] = dot(kl_ref[0, 0, lat_offset(c):lat_offset(c) + kb, :], q_t)

    def consume(c):
        if c == 0:
            pv = lambda p: dot(vtc_ref[0, 0], p[0:n_ctx]) + dot(vtl_ref[0, 0, :, 0:head], p[n_ctx:kb])
        else:
            pv = lambda p: dot(vtl_ref[0, 0, :, lat_offset(c):lat_offset(c) + kb], p)
        _softmax_pv(buf(c), pv, m_sc, l_sc, acc_sc, c == 0)

    def steps(qi, lo, hi):
        for c in range(lo, hi):
            if c + 1 < n_chunks:
                scores(c + 1, qi)
            else:
                scores(0, jnp.minimum(qi + 1, n_q - 1))
            consume(c)

    one_trip = jnp.minimum(pl.program_id(0) + 1, 1)
    scores(0, 0)

    def q_block(qi, carry):
        def first_half(_, carry):
            steps(qi, 0, split)
            return carry
        lax.fori_loop(0, one_trip, first_half, 0)
        steps(qi, split, n_chunks)
        o_ref[0, 0, :, q_cols(qi)] = (acc_sc[...] / l_sc[...]).astype(o_ref.dtype)
        return carry
    lax.fori_loop(0, n_q, q_block, 0)


def _attention_lat(q_t, k_c, vt_c, k_l, vt_l, qb, kb):
    bsz, heads, _, t = q_t.shape
    n_ctx, n_lat = k_c.shape[2], k_l.shape[2]
    n_chunks = (n_ctx + n_lat) // kb
    assert n_chunks * kb == n_ctx + n_lat and n_chunks >= 2 and n_ctx < kb and t % qb == 0
    per_head = lambda shape: pl.BlockSpec((1, 1) + shape, lambda b, h: (b, h, 0, 0))
    return pl.pallas_call(
        functools.partial(_attn_lat_kernel, kb=kb, qb=qb, split=n_chunks // 2),
        grid=(bsz, heads),
        in_specs=[per_head((QK_PAD, t)), per_head((n_ctx, QK_PAD)), per_head((V_DIM, n_ctx)),
                  per_head((n_lat, QK_PAD)), per_head((V_DIM, n_lat))],
        out_specs=per_head((V_DIM, t)),
        out_shape=jax.ShapeDtypeStruct((bsz, heads, V_DIM, t), BF16),
        scratch_shapes=[pltpu.VMEM((kb, qb), F32), pltpu.VMEM((kb, qb), F32), pltpu.VMEM((kb, qb), F32),
                        pltpu.VMEM((1, qb), F32), pltpu.VMEM((1, qb), F32), pltpu.VMEM((V_DIM, qb), F32)],
        compiler_params=_params("arbitrary", "arbitrary"),
        name="attention_lat",
    )(q_t, k_c, vt_c, k_l, vt_l)


def _mla_post_value(ot_ref, gt_ref, w_ref, ga_ref, h_ref):
    og = (ot_ref[0].astype(F32) * gt_ref[0].astype(F32)).astype(BF16)
    y = lax.dot_general(og, w_ref[...], TN_DIMS, preferred_element_type=F32)
    return h_ref[0] + ga_ref[0] * y


def _fno_post_value(y_ref, sg_ref, w_ref, ga_ref, h_ref):
    yg = (y_ref[0].astype(F32) * sg_ref[0].astype(F32)).astype(BF16)
    return h_ref[0] + ga_ref[0] * jnp.dot(yg, w_ref[...], preferred_element_type=F32)


def _mla_post_kernel(ot_ref, gt_ref, w_ref, ga_ref, h_ref, out_ref):
    out_ref[0] = _mla_post_value(ot_ref, gt_ref, w_ref, ga_ref, h_ref)


def _mla_post(o_t, g_t, w_out, ga, h, tm):
    bsz, t, d = h.shape
    return pl.pallas_call(
        _mla_post_kernel,
        grid=(bsz, t // tm),
        in_specs=[
            pl.BlockSpec((1, d, tm), lambda b, i: (b, 0, i)),
            pl.BlockSpec((1, d, tm), lambda b, i: (b, 0, i)),
            pl.BlockSpec((d, d), lambda b, i: (0, 0)),
            pl.BlockSpec((1, 1, d), lambda b, i: (b, 0, 0)),
            pl.BlockSpec((1, tm, d), lambda b, i: (b, i, 0)),
        ],
        out_specs=pl.BlockSpec((1, tm, d), lambda b, i: (b, i, 0)),
        out_shape=jax.ShapeDtypeStruct((bsz, t, d), F32),
        compiler_params=_params("arbitrary", "arbitrary"),
        name="mla_post",
    )(o_t, g_t, w_out, ga, h)


def _fno_pre_body(x, g_ref, sc_ref, sh_ref, w_ref, cs_ref, wre_ref, wim_ref, sg_ref):
    ub = _modulated_norm(x, g_ref, sc_ref, sh_ref)
    proj = jnp.dot(ub, w_ref[...], preferred_element_type=F32)
    d = sg_ref.shape[2]
    sg_ref[0] = _silu(proj[:, d:]).astype(BF16)
    zb = proj[:, :d].astype(BF16)
    for g in range(FOURIER_GROUPS):
        lo, hi = g * GROUP_DIM, (g + 1) * GROUP_DIM
        ab = jnp.dot(zb[:, lo:hi], cs_ref[...], preferred_element_type=F32)
        wre_ref[0, :, lo:hi] = ab[:, :GROUP_DIM].astype(BF16)
        wim_ref[0, :, lo:hi] = ab[:, GROUP_DIM:].astype(BF16)


def _fno_pre_kernel(h_ref, *refs):
    _fno_pre_body(h_ref[0], *refs)


def _mla_post_fno_pre_kernel(ot_ref, gt_ref, wo_ref, ga_ref, h_ref, g_ref, sc_ref, sh_ref, w_ref, cs_ref,
                             hn_ref, wre_ref, wim_ref, sg_ref):
    hn = _mla_post_value(ot_ref, gt_ref, wo_ref, ga_ref, h_ref)
    hn_ref[0] = hn
    _fno_pre_body(hn, g_ref, sc_ref, sh_ref, w_ref, cs_ref, wre_ref, wim_ref, sg_ref)


def _fno_pre(h, g, sc, sh, w_in, cs, tm, post=None):
    bsz, t, d = h.shape
    const = lambda shape: pl.BlockSpec(shape, lambda b, i: (0,) * len(shape))
    vec = pl.BlockSpec((1, 1, d), lambda b, i: (b, 0, 0))
    tok = pl.BlockSpec((1, tm, d), lambda b, i: (b, i, 0))
    out = jax.ShapeDtypeStruct((bsz, t, d), BF16)
    in_specs = [tok, const((1, d)), vec, vec, const(w_in.shape), const(cs.shape)]
    out_specs, out_shape = [tok, tok, tok], [out, out, out]
    args = (h, g, sc, sh, w_in, cs)
    body, name = _fno_pre_kernel, "fno_pre"
    if post is not None:
        o_t, g_t, w_out, ga = post
        tok_t = pl.BlockSpec((1, d, tm), lambda b, i: (b, 0, i))
        in_specs = [tok_t, tok_t, const((d, d)), vec] + in_specs
        out_specs = [tok] + out_specs
        out_shape = [jax.ShapeDtypeStruct((bsz, t, d), F32)] + out_shape
        args = (o_t, g_t, w_out, ga) + args
        body, name = _mla_post_fno_pre_kernel, "mla_post_fno_pre"
    return pl.pallas_call(
        body,
        grid=(bsz, t // tm),
        in_specs=in_specs,
        out_specs=out_specs,
        out_shape=out_shape,
        compiler_params=_params("arbitrary", "arbitrary"),
        name=name,
    )(*args)


def _dft1_kernel(xre_ref, xim_ref, f_ref, twc_ref, tws_ref, zre_ref, zim_ref):
    xre = jnp.swapaxes(xre_ref[0], 0, 1)
    xim = jnp.swapaxes(xim_ref[0], 0, 1)
    n2 = xre.shape[1]
    for j in range(xre.shape[0]):
        x = jnp.concatenate([xre[j], xim[j]], axis=0)
        z = jnp.dot(f_ref[...], x, preferred_element_type=F32)
        zr, zi = z[:n2], z[n2:]
        c, s = twc_ref[j], tws_ref[j]
        zre_ref[0, j] = (zr * c - zi * s).astype(BF16)
        zim_ref[0, j] = (zr * s + zi * c).astype(BF16)


def _dft1(wre, wim, f1, twc, tws, g1, cols):
    bsz, t, d = wre.shape
    n2 = t // DFT_INNER
    assert n2 * DFT_INNER == t and DFT_INNER % g1 == 0 and d % cols == 0
    xre = wre.reshape(bsz, n2, DFT_INNER, d)
    xim = wim.reshape(bsz, n2, DFT_INNER, d)
    blk_in = pl.BlockSpec((1, n2, g1, cols), lambda b, i, j: (b, 0, i, j))
    blk_out = pl.BlockSpec((1, g1, n2, cols), lambda b, i, j: (b, i, 0, j))
    tw = pl.BlockSpec((g1, n2, 1), lambda b, i, j: (i, 0, 0))
    out = jax.ShapeDtypeStruct((bsz, DFT_INNER, n2, d), BF16)
    return pl.pallas_call(
        _dft1_kernel,
        grid=(bsz, DFT_INNER // g1, d // cols),
        in_specs=[blk_in, blk_in, pl.BlockSpec(f1.shape, lambda b, i, j: (0, 0)), tw, tw],
        out_specs=[blk_out, blk_out],
        out_shape=[out, out],
        compiler_params=_params("arbitrary", "arbitrary", "arbitrary"),
        name="dft1",
    )(xre, xim, f1, twc, tws)


def _dft2_kernel(zre_ref, zim_ref, f_ref, y_ref, *, scale):
    zre = jnp.swapaxes(zre_ref[0], 0, 1)
    zim = jnp.swapaxes(zim_ref[0], 0, 1)
    ys = []
    for j in range(zre.shape[0]):
        zc = jnp.concatenate([zre[j], zim[j]], axis=0)
        ys.append(jnp.dot(f_ref[...], zc, preferred_element_type=F32) * scale)
    y_ref[0] = jnp.swapaxes(jnp.stack(ys, axis=0), 0, 1).astype(y_ref.dtype)


def _dft2(zre, zim, f2, g2, cols):
    bsz, n1, n2, d = zre.shape
    assert n2 % g2 == 0 and d % cols == 0
    scale = 1.0 / math.sqrt(n1 * n2 * GROUP_DIM)
    blk = pl.BlockSpec((1, n1, g2, cols), lambda b, i, j: (b, 0, i, j))
    y = pl.pallas_call(
        functools.partial(_dft2_kernel, scale=scale),
        grid=(bsz, n2 // g2, d // cols),
        in_specs=[blk, blk, pl.BlockSpec(f2.shape, lambda b, i, j: (0, 0))],
        out_specs=blk,
        out_shape=jax.ShapeDtypeStruct((bsz, n1, n2, d), BF16),
        compiler_params=_params("arbitrary", "arbitrary", "arbitrary"),
        name="dft2",
    )(zre, zim, f2)
    return y.reshape(bsz, n1 * n2, d)


def _fno_post_kernel(y_ref, sg_ref, w_ref, ga_ref, h_ref, fg_ref, out_ref, *, final):
    hn = _fno_post_value(y_ref, sg_ref, w_ref, ga_ref, h_ref)
    if final:
        hn = _rms(hn) * fg_ref[...]
    out_ref[0] = hn


def _fno_post(y, sg, w_out, ga, h, final_g, tm, final):
    bsz, t, d = h.shape
    tok = pl.BlockSpec((1, tm, d), lambda b, i: (b, i, 0))
    return pl.pallas_call(
        functools.partial(_fno_post_kernel, final=final),
        grid=(bsz, t // tm),
        in_specs=[tok, tok, pl.BlockSpec((d, d), lambda b, i: (0, 0)),
                  pl.BlockSpec((1, 1, d), lambda b, i: (b, 0, 0)), tok,
                  pl.BlockSpec((1, d), lambda b, i: (0, 0))],
        out_specs=tok,
        out_shape=jax.ShapeDtypeStruct((bsz, t, d), F32),
        compiler_params=_params("arbitrary", "arbitrary"),
        name="fno_post",
    )(y, sg, w_out, ga, h, final_g)


def _dft2_post_kernel(zre_ref, zim_ref, f_ref, sg_ref, w_ref, ga_ref, h_ref, fg_ref, out_ref,
                      *, kk, scale, final):
    d = w_ref.shape[0]
    for j in range(kk):
        zc = jnp.concatenate([zre_ref[0, j], zim_ref[0, j]], axis=0)
        y = jnp.dot(f_ref[...], zc, preferred_element_type=F32) * scale
        sg = sg_ref[0, :, j * d:(j + 1) * d].astype(F32)
        o = jnp.dot((y * sg).astype(BF16), w_ref[...], preferred_element_type=F32)
        hn = h_ref[0, :, j * d:(j + 1) * d] + ga_ref[0] * o
        if final:
            hn = _rms(hn) * fg_ref[...]
        out_ref[0, :, j * d:(j + 1) * d] = hn


def _dft2_post(zre, zim, f2, sg, w_out, ga, h, final_g, n1, kk, final):
    bsz, t, d = h.shape
    n2 = t // n1
    zre = zre.reshape(bsz, n2, n1, d)
    zim = zim.reshape(bsz, n2, n1, d)
    zblk = pl.BlockSpec((1, kk, n1, d), lambda b, i: (b, i, 0, 0))
    tok = pl.BlockSpec((1, n1, kk * d), lambda b, i: (b, 0, i))
    scale = 1.0 / math.sqrt(t * GROUP_DIM)
    out = pl.pallas_call(
        functools.partial(_dft2_post_kernel, kk=kk, scale=scale, final=final),
        grid=(bsz, n2 // kk),
        in_specs=[zblk, zblk, pl.BlockSpec(f2.shape, lambda b, i: (0, 0)), tok,
                  pl.BlockSpec((d, d), lambda b, i: (0, 0)),
                  pl.BlockSpec((1, 1, d), lambda b, i: (b, 0, 0)), tok,
                  pl.BlockSpec((1, d), lambda b, i: (0, 0))],
        out_specs=tok,
        out_shape=jax.ShapeDtypeStruct((bsz, n1, n2 * d), F32),
        compiler_params=_params("arbitrary", "arbitrary"),
        name="dft2_post",
    )(zre, zim, f2, sg.reshape(bsz, n1, n2 * d), w_out, ga, h.reshape(bsz, n1, n2 * d), final_g)
    return out.reshape(bsz, t, d)


def _dft_cos_sin(n):
    idx = np.arange(n)
    ang = 2.0 * np.pi * ((idx[:, None] * idx[None, :]) % n) / n
    return np.cos(ang), np.sin(ang)


def _dft_tables(t):
    cn, sn = _dft_cos_sin(GROUP_DIM)
    cs = np.concatenate([cn, sn], axis=1)
    if t <= 256:
        n1, n2 = t, 1
    else:
        n1, n2 = DFT_INNER, t // DFT_INNER
    c1, s1 = _dft_cos_sin(n1)
    f2 = np.concatenate([c1, -s1], axis=1)
    c2, s2 = _dft_cos_sin(n2)
    f1 = np.block([[c2, -s2], [s2, c2]])
    k2 = np.arange(n2)[None, :, None]
    t1 = np.arange(n1)[:, None, None]
    ang = 2.0 * np.pi * ((k2 * t1) % t) / t
    as_f32 = lambda a: jnp.asarray(a.astype(np.float32))
    return dict(n1=n1, n2=n2, cs=as_f32(cs).astype(BF16), f1=as_f32(f1).astype(BF16),
                f2=as_f32(f2).astype(BF16), twc=as_f32(np.cos(ang)), tws=as_f32(np.sin(ang)))


def _rope_tables(n_tok, positional):
    inv_freq = 1.0 / (ROPE_THETA ** (np.arange(0, 2 * ROPE_QUARTER, 2, dtype=np.float64) / (2 * ROPE_QUARTER)))
    pos = np.arange(n_tok)
    scale = 1.0 if positional else 0.0
    ang_r = scale * (pos // GRID_W)[:, None] * inv_freq[None, :]
    ang_c = scale * (pos % GRID_W)[:, None] * inv_freq[None, :]
    cos_r, sin_r, cos_c, sin_c = np.cos(ang_r), np.sin(ang_r), np.cos(ang_c), np.sin(ang_c)
    pad = np.zeros((n_tok, 128 - QK_ROPE))
    rope_t = np.concatenate([cos_r, sin_r, cos_c, sin_c], axis=1).T
    cosk = np.concatenate([cos_r, cos_r, cos_c, cos_c, pad], axis=1)
    sink = np.concatenate([sin_r, sin_r, sin_c, sin_c, pad], axis=1)
    return tuple(jnp.asarray(a.astype(np.float32)) for a in (rope_t, cosk, sink))


def _mla_weights(w_in, g_qa, w_qup, g_kva, w_kvup, w_out):
    d = w_in.shape[0]
    kv_hi = Q_LORA + KV_LORA
    pe_hi = kv_hi + QK_ROPE
    w_pe = w_in[:, kv_hi:pe_hi]
    r = ROPE_QUARTER
    w_rot = jnp.concatenate([-w_pe[:, r:2 * r], w_pe[:, 0:r], -w_pe[:, 3 * r:4 * r], w_pe[:, 2 * r:3 * r]], axis=1)
    zpad = jnp.zeros((d, 128 - QK_ROPE), w_in.dtype)
    wa = jnp.concatenate([w_in[:, :kv_hi], w_pe, zpad, w_rot, zpad], axis=1).astype(BF16)
    wgt = w_in[:, pe_hi:].T.astype(BF16)
    wq = w_qup.reshape(Q_LORA, HEADS, QK_NOPE + QK_ROPE)
    wq = jnp.pad(wq, ((0, 0), (0, 0), (0, QK_PAD - QK_NOPE - QK_ROPE)))
    wqt = wq.reshape(Q_LORA, HEADS * QK_PAD).T.astype(BF16)
    wkv = w_kvup.reshape(KV_LORA, HEADS, QK_NOPE + V_DIM)
    wk = wkv[:, :, :QK_NOPE].reshape(KV_LORA, HEADS * QK_NOPE).astype(BF16)
    wvt = wkv[:, :, QK_NOPE:].reshape(KV_LORA, HEADS * V_DIM).T.astype(BF16)
    return (wa, wgt, g_qa.reshape(1, Q_LORA), wqt, g_kva.reshape(1, KV_LORA), wk, wvt), w_out.astype(BF16)


def kernel(x, c, ctx, c_ctx, norm_g, w_ada, b_ada, mla_w_in, mla_g_qa, mla_w_qup, mla_g_kva,
           mla_w_kvup, mla_w_out, fno_w_in, fno_w_out, final_g):
    bsz, n_tok, d = x.shape
    n_ctx = ctx.shape[1]
    depth = norm_g.shape[0]

    rows = 16
    cond =jnp.concatenate([c, c_ctx[None, :], jnp.zeros((rows - bsz - 1, d), F32)], axis=0)
    mods = _ada(cond, w_ada, b_ada)

    rope_lat = _rope_tables(n_tok, positional=True)
    rope_ctx = _rope_tables(n_ctx, positional=False)
    dft_lat = _dft_tables(n_tok)
    dft_ctx = _dft_tables(n_ctx)
    fg = final_g.reshape(1, d)
    assert depth % 2 == 0, "the last layer's output side (with the final norm) is the Fourier one"

    h_lat, h_ctx, pending = x, ctx, None
    for i in range(depth):
        mixer, idx = i % 2, i // 2
        ctx_later = any(j % 2 == 0 for j in range(i + 1, depth))
        g = norm_g[i].reshape(1, d)
        sh, sc, ga = (mods[i, :, j * d:(j + 1) * d] for j in range(3))
        lat = lambda a: a[:bsz].reshape(bsz, 1, d)
        cx = lambda a: jnp.broadcast_to(a[bsz].reshape(1, 1, d), (bsz, 1, d))
        if mixer == 0:
            wts, w_out = _mla_weights(mla_w_in[idx], mla_g_qa[idx], mla_w_qup[idx], mla_g_kva[idx],
                                      mla_w_kvup[idx], mla_w_out[idx])
            outs = _mla_pre(h_lat, g, lat(sc), lat(sh), wts, rope_lat, tm=512, post=pending)
            if pending is not None:
                h_lat, outs = outs[0], outs[1:]
            qt_l, k_l, vt_l, gt_l = outs
            qt_c, k_c, vt_c, gt_c = _mla_pre(h_ctx, g, cx(sc), cx(sh), wts, rope_ctx, tm=n_ctx)
            o_l = _attention_lat(qt_l, k_c, vt_c, k_l, vt_l, qb=512, kb=768)
            pending = (o_l.reshape(bsz, d, n_tok), gt_l, w_out, lat(ga))
            if ctx_later:
                o_c = _attention(qt_c, [(k_c, vt_c)], qb=n_ctx, kb=n_ctx)
                h_ctx = _mla_post(o_c.reshape(bsz, d, n_ctx), gt_c, w_out, cx(ga), h_ctx, tm=n_ctx)
        else:
            w_in = fno_w_in[idx].astype(BF16)
            w_out = fno_w_out[idx].astype(BF16)
            outs = _fno_pre(h_lat, g, lat(sc), lat(sh), w_in, dft_lat["cs"], tm=512, post=pending)
            if pending is not None:
                h_lat, outs = outs[0], outs[1:]
            wre, wim, sg = outs
            zre, zim = _dft1(wre, wim, dft_lat["f1"], dft_lat["twc"], dft_lat["tws"], g1=16, cols=1024)
            y = _dft2(zre, zim, dft_lat["f2"], g2=16, cols=512)
            pending = (y, sg, w_out, lat(ga))
            if ctx_later:
                wre, wim, sg = _fno_pre(h_ctx, g, cx(sc), cx(sh), w_in, dft_ctx["cs"], tm=n_ctx)
                h_ctx = _dft2_post(wre, wim, dft_ctx["f2"], sg, w_out, cx(ga), h_ctx, fg,
                                   n1=dft_ctx["n1"], kk=1, final=False)
    y, sg, w_out, ga = pending
    return _fno_post(y, sg, w_out, ga, h_lat, fg, tm=1024, final=True)
```
